```python
import math
import jax, jax.numpy as jnp
from jax import lax
import numpy as np

D_MODEL = 1024
BATCH = 1
SEQ = 16384
DEPTH = 1

D_MIX = D_MODEL
MLSTM_WIDTH = D_MIX // 2
N_MLSTM_HEADS = 4
HEAD_DIM = MLSTM_WIDTH // N_MLSTM_HEADS
POOL_WIDTH = D_MIX - MLSTM_WIDTH
POOL_WINDOWS = (2, 4, 8, 16)
N_POOL_GROUPS = len(POOL_WINDOWS)
POOL_GROUP_DIM = POOL_WIDTH // N_POOL_GROUPS
CONV_WIDTH = 4
CHUNK = 64
D_FF = 2816
EPS = 1e-6
OFF_QK = 0
OFF_V = 2 * MLSTM_WIDTH
OFF_O = 3 * MLSTM_WIDTH
OFF_G = 4 * MLSTM_WIDTH
OFF_P = OFF_G + 2 * N_MLSTM_HEADS
D_IN_PROJ = OFF_P + POOL_WIDTH

kernel_name = "hymba_mlstm_pool_macaron"


def rmsnorm(x, g):
    xf = x.astype(jnp.float32)
    y = xf * lax.rsqrt(jnp.mean(xf * xf, axis=-1, keepdims=True) + EPS)
    return (y * g.astype(jnp.float32)).astype(x.dtype)


def swiglu(h, w_gate, w_up, w_down):
    return (jax.nn.silu(h @ w_gate) * (h @ w_up)) @ w_down


def causal_dwconv(u, w, b):
    k_w, c = w.shape
    y = lax.conv_general_dilated(
        u, w[:, None, :].astype(u.dtype), window_strides=(1,),
        padding=[(k_w - 1, 0)], dimension_numbers=('NWC', 'WIO', 'NWC'),
        feature_group_count=c)
    return y + b.astype(u.dtype)


def mlstm_chunkwise(q, k, v, i_pre, f_pre):
    bsz, s, nh, d = q.shape
    nc = s // CHUNK
    to_chunks = lambda t: t.reshape(bsz, nc, CHUNK, nh, d).transpose(0, 3, 1, 2, 4)
    gate_chunks = lambda t: t.reshape(bsz, nc, CHUNK, nh).transpose(0, 3, 1, 2)
    qc = to_chunks(q)
    kc = to_chunks(k) * (d ** -0.5)
    vc = to_chunks(v)
    ig = gate_chunks(i_pre)
    bcum = jnp.cumsum(jax.nn.log_sigmoid(gate_chunks(f_pre)), axis=-1)

    def step(carry, inp):
        c_st, n_st, m_st = carry
        k_c, v_c, b_c, i_c = inp
        g = b_c[..., -1]
        a = g[..., None] - b_c + i_c
        m_new = jnp.maximum(g + m_st, jnp.max(a, axis=-1))
        w = jnp.exp(a - m_new[..., None])
        decay = jnp.exp(g + m_st - m_new)
        c_new = decay[..., None, None] * c_st + jnp.einsum('bhlk,bhlv->bhkv', k_c * w[..., None], v_c)
        n_new = decay[..., None] * n_st + jnp.einsum('bhl,bhlk->bhk', w, k_c)
        return (c_new, n_new, m_new), (c_st, n_st, m_st)

    init = (jnp.zeros((bsz, nh, d, d), jnp.float32),
            jnp.zeros((bsz, nh, d), jnp.float32),
            jnp.zeros((bsz, nh), jnp.float32))
    xs = (jnp.moveaxis(kc, 2, 0), jnp.moveaxis(vc, 2, 0),
          jnp.moveaxis(bcum, 2, 0), jnp.moveaxis(ig, 2, 0))
    _, (c_prev, n_prev, m_prev) = lax.scan(step, init, xs)
    c_prev = jnp.moveaxis(c_prev, 0, 2)
    n_prev = jnp.moveaxis(n_prev, 0, 2)
    m_prev = jnp.moveaxis(m_prev, 0, 2)

    causal = jnp.tril(jnp.ones((CHUNK, CHUNK), dtype=bool))
    inter_log = bcum + m_prev[..., None]
    dlog = bcum[..., :, None] - bcum[..., None, :] + ig[..., None, :]
    dlog = jnp.where(causal, dlog, -jnp.inf)
    m_t = jnp.maximum(inter_log, jnp.max(dlog, axis=-1))
    dw = jnp.exp(dlog - m_t[..., None])
    inter_w = jnp.exp(inter_log - m_t)
    scores = jnp.einsum('bhcld,bhcsd->bhcls', qc, kc) * dw
    num = (jnp.einsum('bhcls,bhcsd->bhcld', scores, vc)
           + inter_w[..., None] * jnp.einsum('bhcld,bhcdv->bhclv', qc, c_prev))
    qn = jnp.sum(scores, axis=-1) + inter_w * jnp.einsum('bhcld,bhcd->bhcl', qc, n_prev)
    den = jnp.maximum(jnp.abs(qn), jnp.exp(-m_t))
    h = num / den[..., None]
    return h.transpose(0, 2, 3, 1, 4).reshape(bsz, s, nh * d)


def multiscale_pool(u, pool_w, pool_scale):
    bsz, s, _ = u.shape
    pos = jnp.arange(1, s + 1, dtype=jnp.int32)
    outs = []
    for gi, win in enumerate(POOL_WINDOWS):
        ug = u[..., gi * POOL_GROUP_DIM:(gi + 1) * POOL_GROUP_DIM].astype(jnp.float32)
        cs = jnp.cumsum(ug, axis=1)
        lag = jnp.pad(cs, ((0, 0), (win, 0), (0, 0)))[:, :s]
        count = jnp.minimum(pos, win).astype(jnp.float32)
        outs.append((cs - lag) / count[None, :, None] - ug)
    pooled = jnp.stack(outs, axis=2)
    mixed = jnp.einsum('bsgc,gcd->bsgd', pooled, pool_w.astype(jnp.float32))
    return mixed.reshape(bsz, s, POOL_WIDTH) * pool_scale.astype(jnp.float32)


def setup_inputs(seed: int = 0) -> dict:
    key = jax.random.key(seed)
    ks = jax.random.split(key, 24)
    nrm = lambda k, shape, scale: jax.random.normal(k, shape, jnp.float32) * scale
    gain = lambda k, shape: 1.0 + 0.1 * jax.random.normal(k, shape, jnp.float32)
    col_scale = jnp.ones((D_IN_PROJ,), jnp.float32).at[OFF_G:OFF_P].set(0.1)
    w_in = nrm(ks[5], (DEPTH, D_MODEL, D_IN_PROJ), D_MODEL ** -0.5) * col_scale
    b_i = nrm(ks[6], (DEPTH, N_MLSTM_HEADS), 0.1)
    b_f = jnp.linspace(3.0, 6.0, N_MLSTM_HEADS, dtype=jnp.float32)[None, :] + nrm(ks[7], (DEPTH, N_MLSTM_HEADS), 0.1)
    return {
        "x": nrm(ks[0], (BATCH, SEQ, D_MODEL), 1.0),
        "ffn1_norm": gain(ks[1], (DEPTH, D_MODEL)),
        "ffn1_w_gate": nrm(ks[2], (DEPTH, D_MODEL, D_FF), D_MODEL ** -0.5),
        "ffn1_w_up": nrm(ks[3], (DEPTH, D_MODEL, D_FF), D_MODEL ** -0.5),
        "ffn1_w_down": nrm(ks[4], (DEPTH, D_FF, D_MODEL), D_FF ** -0.5),
        "mix_norm": gain(ks[8], (DEPTH, D_MODEL)),
        "w_in": w_in,
        "b_gates": jnp.concatenate([b_i, b_f], axis=-1),
        "conv_w": nrm(ks[9], (DEPTH, CONV_WIDTH, 2 * MLSTM_WIDTH), CONV_WIDTH ** -0.5),
        "conv_b": nrm(ks[10], (DEPTH, 2 * MLSTM_WIDTH), 0.01),
        "mh_norm": gain(ks[11], (DEPTH, MLSTM_WIDTH)),
        "pool_w": nrm(ks[12], (DEPTH, N_POOL_GROUPS, POOL_GROUP_DIM, POOL_GROUP_DIM), POOL_GROUP_DIM ** -0.5),
        "pool_scale": gain(ks[13], (DEPTH, POOL_WIDTH)),
        "w_out": nrm(ks[14], (DEPTH, D_MIX, D_MODEL), D_MIX ** -0.5),
        "ffn2_norm": gain(ks[15], (DEPTH, D_MODEL)),
        "ffn2_w_gate": nrm(ks[16], (DEPTH, D_MODEL, D_FF), D_MODEL ** -0.5),
        "ffn2_w_up": nrm(ks[17], (DEPTH, D_MODEL, D_FF), D_MODEL ** -0.5),
        "ffn2_w_down": nrm(ks[18], (DEPTH, D_FF, D_MODEL), D_FF ** -0.5),
        "final_norm": gain(ks[19], (D_MODEL,)),
    }


def reference(x, ffn1_norm, ffn1_w_gate, ffn1_w_up, ffn1_w_down, mix_norm, w_in, b_gates,
              conv_w, conv_b, mh_norm, pool_w, pool_scale, w_out, ffn2_norm, ffn2_w_gate,
              ffn2_w_up, ffn2_w_down, final_norm):
    bsz, s, _ = x.shape
    for l in range(DEPTH):
        x = x + 0.5 * swiglu(rmsnorm(x, ffn1_norm[l]), ffn1_w_gate[l], ffn1_w_up[l], ffn1_w_down[l])

        h = rmsnorm(x, mix_norm[l])
        proj = h @ w_in[l]
        qk = jax.nn.silu(causal_dwconv(proj[..., OFF_QK:OFF_V], conv_w[l], conv_b[l]))
        q = qk[..., :MLSTM_WIDTH].astype(jnp.float32).reshape(bsz, s, N_MLSTM_HEADS, HEAD_DIM)
        k = qk[..., MLSTM_WIDTH:].astype(jnp.float32).reshape(bsz, s, N_MLSTM_HEADS, HEAD_DIM)
        v = proj[..., OFF_V:OFF_O].astype(jnp.float32).reshape(bsz, s, N_MLSTM_HEADS, HEAD_DIM)
        o_gate = jax.nn.sigmoid(proj[..., OFF_O:OFF_G].astype(jnp.float32))
        gates = proj[..., OFF_G:OFF_P].astype(jnp.float32) + b_gates[l].astype(jnp.float32)
        i_pre = gates[..., :N_MLSTM_HEADS]
        f_pre = gates[..., N_MLSTM_HEADS:]

        hm = mlstm_chunkwise(q, k, v, i_pre, f_pre).reshape(bsz, s, N_MLSTM_HEADS, HEAD_DIM)
        hm = hm * lax.rsqrt(jnp.mean(hm * hm, axis=-1, keepdims=True) + EPS)
        hm = o_gate * (hm.reshape(bsz, s, MLSTM_WIDTH) * mh_norm[l].astype(jnp.float32))

        hp = multiscale_pool(proj[..., OFF_P:], pool_w[l], pool_scale[l])

        mix = jnp.concatenate([hm, hp], axis=-1).astype(x.dtype) @ w_out[l]
        x = x + mix

        x = x + 0.5 * swiglu(rmsnorm(x, ffn2_norm[l]), ffn2_w_gate[l], ffn2_w_up[l], ffn2_w_down[l])
    return rmsnorm(x, final_norm)
```

```python
import functools

import jax
import jax.numpy as jnp
from jax import lax
from jax.experimental import pallas as pl
from jax.experimental.pallas import tpu as pltpu

EPS = 1e-6
N_HEADS = 4
HEAD_DIM = 128
POOL_WINDOWS = (2, 4, 8, 16)
POOL_GROUP_DIM = 128
CONV_WIDTH = 4

LANES = 128
HALO = 16
V7X_VMEM_BYTES = 64 * 1024 * 1024

FFN_ROWS = 512
FFN_COLS = 256
MIX_ROWS = 512
MIX_CHUNK = 256

_F32 = jnp.float32
_BF16 = jnp.bfloat16


def _rmsnorm(x, g):
    y = x * lax.rsqrt(jnp.mean(x * x, axis=-1, keepdims=True) + EPS)
    return y * g


def _const_spec(shape):
    return pl.BlockSpec(shape, lambda i: (0,) * len(shape))


def _vmem_limit(nbytes):
    return int(min(nbytes * 1.25 + (8 << 20), V7X_VMEM_BYTES - (4 << 20)))


def _ffn_kernel(x_ref, g_ref, wg_ref, wu_ref, wd_ref, *rest, cols, final):
    if final:
        fg_ref, o_ref, h_scr, act_scr = rest
    else:
        o_ref, h_scr, act_scr = rest
    d_ff = wg_ref.shape[1]
    h_scr[...] = _rmsnorm(x_ref[...], g_ref[...]).astype(_BF16)
    for j in range(d_ff // cols):
        sl = slice(j * cols, (j + 1) * cols)
        h = h_scr[...]
        gate = jnp.dot(h, wg_ref[:, sl], preferred_element_type=_F32)
        up = jnp.dot(h, wu_ref[:, sl], preferred_element_type=_F32)
        act_scr[:, sl] = (gate * jax.nn.sigmoid(gate) * up).astype(_BF16)
    down = jnp.dot(act_scr[...], wd_ref[...], preferred_element_type=_F32)
    y = x_ref[...] + 0.5 * down
    if final:
        y = _rmsnorm(y, fg_ref[...])
    o_ref[...] = y


def _ffn_call(x, g, wg, wu, wd, final_g=None):
    s, d = x.shape
    d_ff = wg.shape[1]
    rows, cols = FFN_ROWS, FFN_COLS
    assert s % rows == 0 and d_ff % cols == 0
    final = final_g is not None
    row_spec = pl.BlockSpec((rows, d), lambda i: (i, 0))
    in_specs = [row_spec, _const_spec((1, d)), _const_spec((d, d_ff)), _const_spec((d, d_ff)),
                _const_spec((d_ff, d))]
    args = [x, g, wg, wu, wd]
    if final:
        in_specs.append(_const_spec((1, d)))
        args.append(final_g)
    vmem = 2 * 3 * d * d_ff * 2 + 4 * rows * d * 4 + rows * d * 2 + rows * d_ff * 2 + 4 * rows * cols * 4
    return pl.pallas_call(
        functools.partial(_ffn_kernel, cols=cols, final=final),
        grid=(s // rows,),
        in_specs=in_specs,
        out_specs=row_spec,
        out_shape=jax.ShapeDtypeStruct((s, d), _F32),
        scratch_shapes=[pltpu.VMEM((rows, d), _BF16), pltpu.VMEM((rows, d_ff), _BF16)],
        compiler_params=pltpu.CompilerParams(dimension_semantics=("arbitrary",),
                                             vmem_limit_bytes=_vmem_limit(vmem)),
        name="swiglu_final" if final else "swiglu",
    )(*args)


def _mlstm_chunk(q, k, v, ig_col, ig_row, b_col, b_row, c_prev, n_prev, m_prev):
    length = q.shape[0]
    qb, kb, vb = q.astype(_BF16), k.astype(_BF16), v.astype(_BF16)
    row_id = lax.broadcasted_iota(jnp.int32, (length, length), 0)
    col_id = lax.broadcasted_iota(jnp.int32, (length, length), 1)
    dlog = jnp.where(row_id >= col_id, b_col + (ig_row - b_row), -jnp.inf)
    inter_log = b_col + m_prev
    m_t = jnp.maximum(inter_log, jnp.max(dlog, axis=1, keepdims=True))
    dw = jnp.exp(dlog - m_t)
    inter_w = jnp.exp(inter_log - m_t)
    scores = lax.dot_general(qb, kb, (((1,), (1,)), ((), ())), preferred_element_type=_F32) * dw
    num = (jnp.dot(scores.astype(_BF16), vb, preferred_element_type=_F32)
           + inter_w * jnp.dot(qb, c_prev.astype(_BF16), preferred_element_type=_F32))
    qn = (jnp.sum(scores, axis=1, keepdims=True)
          + inter_w * jnp.sum(q * n_prev, axis=1, keepdims=True))
    den = jnp.maximum(jnp.abs(qn), jnp.exp(-m_t))
    h = num / den

    g = b_col[length - 1:length, :]
    a = g - b_col + ig_col
    m_new = jnp.maximum(g + m_prev, jnp.max(a, axis=0, keepdims=True))
    w = jnp.exp(a - m_new)
    decay = jnp.exp(g + m_prev - m_new)
    kw = k * w
    c_new = decay * c_prev + lax.dot_general(kw.astype(_BF16), vb, (((0,), (0,)), ((), ())),
                                             preferred_element_type=_F32)
    n_new = decay * n_prev + jnp.sum(kw, axis=0, keepdims=True)
    return h, c_new, n_new, m_new


def _mixer_kernel(x_ref, g_ref, wqkvo_ref, wgate_ref, wpool_ref, bgate_ref, convw_ref, convb_ref,
                  mhn_ref, poolw_ref, pscale_ref, wout_ref, o_ref,
                  proj_scr, u_scr, qk_scr, mix_scr, c_scr, n_scr, m_scr, *, chunk):
    rows = x_ref.shape[0]
    width = N_HEADS * HEAD_DIM
    step = pl.program_id(0)

    @pl.when(step == 0)
    def _():
        proj_scr[0:HALO, :] = jnp.zeros((HALO, proj_scr.shape[1]), _F32)
        u_scr[0:HALO, :] = jnp.zeros((HALO, u_scr.shape[1]), _F32)
        c_scr[...] = jnp.zeros(c_scr.shape, _F32)
        n_scr[...] = jnp.zeros(n_scr.shape, _F32)
        m_scr[...] = jnp.zeros(m_scr.shape, _F32)

    x = x_ref[...]
    hb = _rmsnorm(x, g_ref[...]).astype(_BF16)

    proj_scr[HALO:, :] = jnp.dot(hb, wqkvo_ref[...], preferred_element_type=_F32)
    u_scr[HALO:, :] = jnp.dot(hb, wpool_ref[...], preferred_element_type=_F32)
    gates = jnp.dot(hb, wgate_ref[...], preferred_element_type=_F32) + bgate_ref[...]

    conv = convb_ref[...] + convw_ref[CONV_WIDTH - 1:CONV_WIDTH, :] * proj_scr[HALO:, 0:2 * width]
    for j in range(CONV_WIDTH - 1):
        off = HALO - (CONV_WIDTH - 1) + j
        conv = conv + convw_ref[j:j + 1, :] * proj_scr[off:off + rows, 0:2 * width]
    qk_scr[...] = conv * jax.nn.sigmoid(conv)

    lane = lax.broadcasted_iota(jnp.int32, gates.shape, 1)
    gcol = jnp.where(lane < N_HEADS, gates, jax.nn.log_sigmoid(gates))
    grow = gcol.T
    t_in_chunk = lax.broadcasted_iota(jnp.int32, grow.shape, 1) % chunk
    csum = grow
    shift = 1
    while shift < chunk:
        csum = csum + jnp.where(t_in_chunk >= shift, pltpu.roll(csum, shift, axis=1), 0.0)
        shift *= 2
    sub = lax.broadcasted_iota(jnp.int32, grow.shape, 0)
    grow = jnp.where(sub < N_HEADS, grow, csum)
    gcol = grow.T

    for c in range(rows // chunk):
        rs = slice(c * chunk, (c + 1) * chunk)
        for hd in range(N_HEADS):
            hs = slice(hd * HEAD_DIM, (hd + 1) * HEAD_DIM)
            rh = slice(HALO + c * chunk, HALO + (c + 1) * chunk)
            q = qk_scr[rs, hs]
            k = qk_scr[rs, width + hd * HEAD_DIM: width + (hd + 1) * HEAD_DIM] * (HEAD_DIM ** -0.5)
            v = proj_scr[rh, 2 * width + hd * HEAD_DIM: 2 * width + (hd + 1) * HEAD_DIM]
            og = proj_scr[rh, 3 * width + hd * HEAD_DIM: 3 * width + (hd + 1) * HEAD_DIM]
            h, c_new, n_new, m_new = _mlstm_chunk(
                q, k, v,
                gcol[rs, hd:hd + 1], grow[hd:hd + 1, rs],
                gcol[rs, N_HEADS + hd:N_HEADS + hd + 1], grow[N_HEADS + hd:N_HEADS + hd + 1, rs],
                c_scr[hd], n_scr[hd], m_scr[hd])
            c_scr[hd] = c_new
            n_scr[hd] = n_new
            m_scr[hd] = m_new
            h = h * lax.rsqrt(jnp.mean(h * h, axis=-1, keepdims=True) + EPS)
            mix_scr[rs, hs] = (jax.nn.sigmoid(og) * (h * mhn_ref[:, hs])).astype(_BF16)

    pos = step * rows + lax.broadcasted_iota(jnp.int32, (rows, 1), 0) + 1
    for gi, win in enumerate(POOL_WINDOWS):
        gs = slice(gi * POOL_GROUP_DIM, (gi + 1) * POOL_GROUP_DIM)
        cur = u_scr[HALO:, gs]
        total = cur
        for lag in range(1, win):
            total = total + u_scr[HALO - lag:HALO - lag + rows, gs]
        count = jnp.minimum(pos, win).astype(_F32)
        pooled = total / count - cur
        mixed = jnp.dot(pooled.astype(_BF16), poolw_ref[gi], preferred_element_type=_F32)
        mix_scr[:, width + gi * POOL_GROUP_DIM: width + (gi + 1) * POOL_GROUP_DIM] = (
            mixed * pscale_ref[:, gs]).astype(_BF16)

    proj_scr[0:HALO, :] = proj_scr[rows:rows + HALO, :]
    u_scr[0:HALO, :] = u_scr[rows:rows + HALO, :]

    o_ref[...] = x + jnp.dot(mix_scr[...], wout_ref[...], preferred_element_type=_F32)


def _mixer_call(x, g, wqkvo, wgate, wpool, bgate, convw, convb, mhn, poolw, pscale, wout):
    s, d = x.shape
    rows, chunk = MIX_ROWS, MIX_CHUNK
    width = N_HEADS * HEAD_DIM
    assert s % rows == 0 and rows % chunk == 0 and chunk % LANES == 0
    row_spec = pl.BlockSpec((rows, d), lambda i: (i, 0))
    consts = [g, wqkvo, wgate, wpool, bgate, convw, convb, mhn, poolw, pscale, wout]
    in_specs = [row_spec] + [_const_spec(a.shape) for a in consts]
    vmem = (2 * sum(a.size * a.dtype.itemsize for a in consts) + 4 * rows * d * 4
            + (rows + HALO) * 5 * width * 4 + rows * 2 * width * 6 + 8 * rows * 2 * width * 4)
    return pl.pallas_call(
        functools.partial(_mixer_kernel, chunk=chunk),
        grid=(s // rows,),
        in_specs=in_specs,
        out_specs=row_spec,
        out_shape=jax.ShapeDtypeStruct((s, d), _F32),
        scratch_shapes=[
            pltpu.VMEM((rows + HALO, 4 * width), _F32),
            pltpu.VMEM((rows + HALO, width), _F32),
            pltpu.VMEM((rows, 2 * width), _F32),
            pltpu.VMEM((rows, 2 * width), _BF16),
            pltpu.VMEM((N_HEADS, HEAD_DIM, HEAD_DIM), _F32),
            pltpu.VMEM((N_HEADS, 1, HEAD_DIM), _F32),
            pltpu.VMEM((N_HEADS, 1, 1), _F32),
        ],
        compiler_params=pltpu.CompilerParams(dimension_semantics=("arbitrary",),
                                             vmem_limit_bytes=_vmem_limit(vmem)),
        name="mixer",
    )(x, *consts)


def kernel(x, ffn1_norm, ffn1_w_gate, ffn1_w_up, ffn1_w_down, mix_norm, w_in, b_gates, conv_w, conv_b,
           mh_norm, pool_w, pool_scale, w_out, ffn2_norm, ffn2_w_gate, ffn2_w_up, ffn2_w_down, final_norm):
    bsz, s, d = x.shape
    depth = ffn1_norm.shape[0]
    assert depth >= 1
    width = N_HEADS * HEAD_DIM
    off_g = 4 * width
    off_p = off_g + 2 * N_HEADS
    bf = lambda a: a.astype(_BF16)
    outs = []
    for b in range(bsz):
        xb = x[b]
        for l in range(depth):
            xb = _ffn_call(xb, ffn1_norm[l][None, :], bf(ffn1_w_gate[l]), bf(ffn1_w_up[l]), bf(ffn1_w_down[l]))
            wgate = jnp.pad(w_in[l][:, off_g:off_p], ((0, 0), (0, LANES - 2 * N_HEADS)))
            bgate = jnp.pad(b_gates[l], (0, LANES - 2 * N_HEADS))[None, :]
            xb = _mixer_call(
                xb, mix_norm[l][None, :], bf(w_in[l][:, :off_g]), bf(wgate), bf(w_in[l][:, off_p:]), bgate,
                conv_w[l], conv_b[l][None, :], mh_norm[l][None, :], bf(pool_w[l]), pool_scale[l][None, :],
                bf(w_out[l]))
            last = l == depth - 1
            xb = _ffn_call(xb, ffn2_norm[l][None, :], bf(ffn2_w_gate[l]), bf(ffn2_w_up[l]), bf(ffn2_w_down[l]),
                           final_g=final_norm[None, :] if last else None)
        outs.append(xb)
    return jnp.stack(outs, axis=0)
```

```python
import functools

import jax
import jax.numpy as jnp
from jax import lax
from jax.experimental import pallas as pl
from jax.experimental.pallas import tpu as pltpu

EPS = 1e-6
N_HEADS = 4
HEAD_DIM = 128
POOL_WINDOWS = (2, 4, 8, 16)
POOL_GROUP_DIM = 128
CONV_WIDTH = 4

LANES = 128
HALO = 16
V7X_VMEM_BYTES = 64 * 1024 * 1024

FFN_ROWS = 512
FFN_COLS = 256
MIX_ROWS = 512
MIX_CHUNK = 256
CONV_BLOCK = 32
POOL_BLOCK = 128

_F32 = jnp.float32
_BF16 = jnp.bfloat16


def _rmsnorm(x, g):
    y = x * lax.rsqrt(jnp.mean(x * x, axis=-1, keepdims=True) + EPS)
    return y * g


def _const_spec(shape):
    return pl.BlockSpec(shape, lambda i: (0,) * len(shape))


def _vmem_limit(nbytes):
    return int(min(nbytes * 1.25 + (8 << 20), V7X_VMEM_BYTES - (4 << 20)))


def _ffn_kernel(x_ref, g_ref, wg_ref, wu_ref, wd_ref, *rest, cols, final):
    if final:
        fg_ref, o_ref, h_scr, act_scr = rest
    else:
        o_ref, h_scr, act_scr = rest
    d_ff = wg_ref.shape[1]
    h_scr[...] = _rmsnorm(x_ref[...], g_ref[...]).astype(_BF16)
    for j in range(d_ff // cols):
        sl = slice(j * cols, (j + 1) * cols)
        h = h_scr[...]
        gate = jnp.dot(h, wg_ref[:, sl], preferred_element_type=_F32)
        up = jnp.dot(h, wu_ref[:, sl], preferred_element_type=_F32)
        act_scr[:, sl] = (gate * jax.nn.sigmoid(gate) * up).astype(_BF16)
    down = jnp.dot(act_scr[...], wd_ref[...], preferred_element_type=_F32)
    y = x_ref[...] + 0.5 * down
    if final:
        y = _rmsnorm(y, fg_ref[...])
    o_ref[...] = y


def _ffn_call(x, g, wg, wu, wd, final_g=None):
    s, d = x.shape
    d_ff = wg.shape[1]
    rows, cols = FFN_ROWS, FFN_COLS
    assert s % rows == 0 and d_ff % cols == 0
    final = final_g is not None
    row_spec = pl.BlockSpec((rows, d), lambda i: (i, 0))
    in_specs = [row_spec, _const_spec((1, d)), _const_spec((d, d_ff)), _const_spec((d, d_ff)),
                _const_spec((d_ff, d))]
    args = [x, g, wg, wu, wd]
    if final:
        in_specs.append(_const_spec((1, d)))
        args.append(final_g)
    vmem = 2 * 3 * d * d_ff * 2 + 4 * rows * d * 4 + rows * d * 2 + rows * d_ff * 2 + 4 * rows * cols * 4
    return pl.pallas_call(
        functools.partial(_ffn_kernel, cols=cols, final=final),
        grid=(s // rows,),
        in_specs=in_specs,
        out_specs=row_spec,
        out_shape=jax.ShapeDtypeStruct((s, d), _F32),
        scratch_shapes=[pltpu.VMEM((rows, d), _BF16), pltpu.VMEM((rows, d_ff), _BF16)],
        compiler_params=pltpu.CompilerParams(dimension_semantics=("arbitrary",),
                                             vmem_limit_bytes=_vmem_limit(vmem)),
        name="swiglu_final" if final else "swiglu",
    )(*args)


def _mlstm_chunk(q, k, v, ig_col, ig_row, b_col, b_row, c_prev, n_prev, m_prev):
    length = q.shape[0]
    qb, kb, vb = q.astype(_BF16), k.astype(_BF16), v.astype(_BF16)
    row_id = lax.broadcasted_iota(jnp.int32, (length, length), 0)
    col_id = lax.broadcasted_iota(jnp.int32, (length, length), 1)
    dlog = jnp.where(row_id >= col_id, b_col + (ig_row - b_row), -jnp.inf)
    inter_log = b_col + m_prev
    m_t = jnp.maximum(inter_log, jnp.max(dlog, axis=1, keepdims=True))
    dw = jnp.exp(dlog - m_t)
    inter_w = jnp.exp(inter_log - m_t)
    scores = lax.dot_general(qb, kb, (((1,), (1,)), ((), ())), preferred_element_type=_F32) * dw
    num = (jnp.dot(scores.astype(_BF16), vb, preferred_element_type=_F32)
           + inter_w * jnp.dot(qb, c_prev.astype(_BF16), preferred_element_type=_F32))
    qn = (jnp.sum(scores, axis=1, keepdims=True)
          + inter_w * jnp.sum(q * n_prev, axis=1, keepdims=True))
    den = jnp.maximum(jnp.abs(qn), jnp.exp(-m_t))
    h = num / den

    g = b_col[length - 1:length, :]
    a = g - b_col + ig_col
    m_new = jnp.maximum(g + m_prev, jnp.max(a, axis=0, keepdims=True))
    w = jnp.exp(a - m_new)
    decay = jnp.exp(g + m_prev - m_new)
    kw = k * w
    c_new = decay * c_prev + lax.dot_general(kw.astype(_BF16), vb, (((0,), (0,)), ((), ())),
                                             preferred_element_type=_F32)
    n_new = decay * n_prev + jnp.sum(kw, axis=0, keepdims=True)
    return h, c_new, n_new, m_new


def _project(x, g_ref, win_ref, proj_ref):
    hb = _rmsnorm(x, g_ref[...]).astype(_BF16)
    proj_ref[HALO:, :] = jnp.dot(hb, win_ref[...], preferred_element_type=_F32)


def _mix_tile(proj_ref, x_ref, o_ref, row0, first_pos, *, bgate_ref, convw_ref, convb_ref, mhn_ref, poolw_ref,
              pscale_ref, wout_ref, qk_scr, mix_scr, c_scr, n_scr, m_scr, rows, chunk):
    width = N_HEADS * HEAD_DIM
    off_pool = 4 * width
    off_gate = off_pool + len(POOL_WINDOWS) * POOL_GROUP_DIM

    for rb in range(rows // CONV_BLOCK):
        base = HALO + rb * CONV_BLOCK
        conv = convb_ref[...] + convw_ref[CONV_WIDTH - 1:CONV_WIDTH, :] * proj_ref[base:base + CONV_BLOCK, 0:2 * width]
        for j in range(CONV_WIDTH - 1):
            off = base - (CONV_WIDTH - 1) + j
            conv = conv + convw_ref[j:j + 1, :] * proj_ref[off:off + CONV_BLOCK, 0:2 * width]
        qk_scr[rb * CONV_BLOCK:(rb + 1) * CONV_BLOCK, :] = conv * jax.nn.sigmoid(conv)

    gates = proj_ref[HALO:, off_gate:off_gate + LANES] + bgate_ref[...]
    g8 = gates.T[0:2 * N_HEADS, :]
    sub = lax.broadcasted_iota(jnp.int32, g8.shape, 0)
    csum = jax.nn.log_sigmoid(g8)
    t_in_chunk = lax.broadcasted_iota(jnp.int32, g8.shape, 1) % chunk
    shift = 1
    while shift < chunk:
        csum = csum + jnp.where(t_in_chunk >= shift, pltpu.roll(csum, shift, axis=1), 0.0)
        shift *= 2
    grow = jnp.where(sub < N_HEADS, g8, csum)
    gcol = jnp.concatenate([grow, jnp.zeros((LANES - 2 * N_HEADS, rows), _F32)], axis=0).T

    for c in range(rows // chunk):
        rs = slice(c * chunk, (c + 1) * chunk)
        rh = slice(HALO + c * chunk, HALO + (c + 1) * chunk)
        for hd in range(N_HEADS):
            hs = slice(hd * HEAD_DIM, (hd + 1) * HEAD_DIM)
            q = qk_scr[rs, hs]
            k = qk_scr[rs, width + hd * HEAD_DIM: width + (hd + 1) * HEAD_DIM] * (HEAD_DIM ** -0.5)
            v = proj_ref[rh, 2 * width + hd * HEAD_DIM: 2 * width + (hd + 1) * HEAD_DIM]
            og = proj_ref[rh, 3 * width + hd * HEAD_DIM: 3 * width + (hd + 1) * HEAD_DIM]
            h, c_new, n_new, m_new = _mlstm_chunk(
                q, k, v,
                gcol[rs, hd:hd + 1], grow[hd:hd + 1, rs],
                gcol[rs, N_HEADS + hd:N_HEADS + hd + 1], grow[N_HEADS + hd:N_HEADS + hd + 1, rs],
                c_scr[hd], n_scr[hd], m_scr[hd])
            c_scr[hd] = c_new
            n_scr[hd] = n_new
            m_scr[hd] = m_new
            h = h * lax.rsqrt(jnp.mean(h * h, axis=-1, keepdims=True) + EPS)
            mix_scr[rs, hs] = (jax.nn.sigmoid(og) * (h * mhn_ref[:, hs])).astype(_BF16)

    for rb in range(rows // POOL_BLOCK):
        r0 = rb * POOL_BLOCK
        pos = first_pos + r0 + lax.broadcasted_iota(jnp.int32, (POOL_BLOCK, 1), 0) + 1
        for gi, win in enumerate(POOL_WINDOWS):
            cs = slice(off_pool + gi * POOL_GROUP_DIM, off_pool + (gi + 1) * POOL_GROUP_DIM)
            total = proj_ref[r0:r0 + HALO + POOL_BLOCK, cs]
            span = 1
            while span < win:
                total = total + pltpu.roll(total, span, axis=0)
                span *= 2
            inv_count = 1.0 / jnp.minimum(pos, win).astype(_F32)
            pooled = total[HALO:, :] * inv_count - proj_ref[HALO + r0:HALO + r0 + POOL_BLOCK, cs]
            mixed = jnp.dot(pooled.astype(_BF16), poolw_ref[gi], preferred_element_type=_F32)
            mix_scr[r0:r0 + POOL_BLOCK, width + gi * POOL_GROUP_DIM: width + (gi + 1) * POOL_GROUP_DIM] = (
                mixed * pscale_ref[:, gi * POOL_GROUP_DIM:(gi + 1) * POOL_GROUP_DIM]).astype(_BF16)

    o_ref[row0:row0 + rows, :] = x_ref[row0:row0 + rows, :] + jnp.dot(
        mix_scr[...], wout_ref[...], preferred_element_type=_F32)


def _mixer_kernel(x_ref, xn_ref, g_ref, win_ref, bgate_ref, convw_ref, convb_ref, mhn_ref, poolw_ref,
                  pscale_ref, wout_ref, o_ref, proj_a, proj_b, qk_scr, mix_scr, c_scr, n_scr, m_scr, *, chunk):
    rows = xn_ref.shape[0]
    step = pl.program_id(0)
    tile = functools.partial(
        _mix_tile, bgate_ref=bgate_ref, convw_ref=convw_ref, convb_ref=convb_ref, mhn_ref=mhn_ref,
        poolw_ref=poolw_ref, pscale_ref=pscale_ref, wout_ref=wout_ref, qk_scr=qk_scr, mix_scr=mix_scr,
        c_scr=c_scr, n_scr=n_scr, m_scr=m_scr, rows=rows, chunk=chunk)

    @pl.when(step == 0)
    def _():
        proj_a[0:HALO, :] = jnp.zeros((HALO, proj_a.shape[1]), _F32)
        c_scr[...] = jnp.zeros(c_scr.shape, _F32)
        n_scr[...] = jnp.zeros(n_scr.shape, _F32)
        m_scr[...] = jnp.zeros(m_scr.shape, _F32)
        _project(x_ref[0:rows, :], g_ref, win_ref, proj_a)

    _project(x_ref[rows:2 * rows, :], g_ref, win_ref, proj_b)
    proj_b[0:HALO, :] = proj_a[rows:rows + HALO, :]
    tile(proj_a, x_ref, o_ref, 0, step * (2 * rows))
    _project(xn_ref[...], g_ref, win_ref, proj_a)
    proj_a[0:HALO, :] = proj_b[rows:rows + HALO, :]
    tile(proj_b, x_ref, o_ref, rows, step * (2 * rows) + rows)


def _mixer_call(x, g, win, bgate, convw, convb, mhn, poolw, pscale, wout):
    s, d = x.shape
    rows, chunk = MIX_ROWS, MIX_CHUNK
    width = N_HEADS * HEAD_DIM
    cols = win.shape[1]
    assert s % (2 * rows) == 0 and rows % chunk == 0 and chunk % LANES == 0
    assert rows % CONV_BLOCK == 0 and rows % POOL_BLOCK == 0
    n_steps = s // (2 * rows)
    pair_spec = pl.BlockSpec((2 * rows, d), lambda i: (i, 0))
    next_spec = pl.BlockSpec((rows, d), lambda i: (jnp.minimum(2 * i + 2, 2 * n_steps - 1), 0))
    consts = [g, win, bgate, convw, convb, mhn, poolw, pscale, wout]
    in_specs = [pair_spec, next_spec] + [_const_spec(a.shape) for a in consts]
    vmem = (2 * sum(a.size * a.dtype.itemsize for a in consts) + 10 * rows * d * 4
            + 2 * (rows + HALO) * cols * 4 + rows * 2 * width * 6 + 4 * rows * cols * 4)
    return pl.pallas_call(
        functools.partial(_mixer_kernel, chunk=chunk),
        grid=(n_steps,),
        in_specs=in_specs,
        out_specs=pair_spec,
        out_shape=jax.ShapeDtypeStruct((s, d), _F32),
        scratch_shapes=[
            pltpu.VMEM((rows + HALO, cols), _F32),
            pltpu.VMEM((rows + HALO, cols), _F32),
            pltpu.VMEM((rows, 2 * width), _F32),
            pltpu.VMEM((rows, 2 * width), _BF16),
            pltpu.VMEM((N_HEADS, HEAD_DIM, HEAD_DIM), _F32),
            pltpu.VMEM((N_HEADS, 1, HEAD_DIM), _F32),
            pltpu.VMEM((N_HEADS, 1, 1), _F32),
        ],
        compiler_params=pltpu.CompilerParams(dimension_semantics=("arbitrary",),
                                             vmem_limit_bytes=_vmem_limit(vmem)),
        name="mixer",
    )(x, x, *consts)


def kernel(x, ffn1_norm, ffn1_w_gate, ffn1_w_up, ffn1_w_down, mix_norm, w_in, b_gates, conv_w, conv_b,
           mh_norm, pool_w, pool_scale, w_out, ffn2_norm, ffn2_w_gate, ffn2_w_up, ffn2_w_down, final_norm):
    bsz, s, d = x.shape
    depth = ffn1_norm.shape[0]
    assert depth >= 1
    width = N_HEADS * HEAD_DIM
    off_g = 4 * width
    off_p = off_g + 2 * N_HEADS
    bf = lambda a: a.astype(_BF16)
    outs = []
    for b in range(bsz):
        xb = x[b]
        for l in range(depth):
            xb = _ffn_call(xb, ffn1_norm[l][None, :], bf(ffn1_w_gate[l]), bf(ffn1_w_up[l]), bf(ffn1_w_down[l]))
            win = jnp.concatenate(
                [w_in[l][:, :off_g], w_in[l][:, off_p:],
                 jnp.pad(w_in[l][:, off_g:off_p], ((0, 0), (0, LANES - 2 * N_HEADS)))], axis=1)
            bgate = jnp.pad(b_gates[l], (0, LANES - 2 * N_HEADS))[None, :]
            xb = _mixer_call(
                xb, mix_norm[l][None, :], bf(win), bgate, conv_w[l], conv_b[l][None, :], mh_norm[l][None, :],
                bf(pool_w[l]), pool_scale[l][None, :], bf(w_out[l]))
            last = l == depth - 1
            xb = _ffn_call(xb, ffn2_norm[l][None, :], bf(ffn2_w_gate[l]), bf(ffn2_w_up[l]), bf(ffn2_w_down[l]),
                           final_g=final_norm[None, :] if last else None)
        outs.append(xb)
    return jnp.stack(outs, axis=0)
```

```python
import functools

import jax
import jax.numpy as jnp
from jax import lax
from jax.experimental import pallas as pl
from jax.experimental.pallas import tpu as pltpu

EPS = 1e-6
N_HEADS = 4
HEAD_DIM = 128
POOL_WINDOWS = (2, 4, 8, 16)
POOL_GROUP_DIM = 128
CONV_WIDTH = 4

LANES = 128
HALO = 16
V7X_VMEM_BYTES = 64 * 1024 * 1024

FFN_ROWS = 512
FFN_COLS = 256
MIX_ROWS = 512
MIX_CHUNK = 256
CONV_BLOCK = 32
POOL_BLOCK = 128

_F32 = jnp.float32
_BF16 = jnp.bfloat16


def _rmsnorm(x, g):
    y = x * lax.rsqrt(jnp.mean(x * x, axis=-1, keepdims=True) + EPS)
    return y * g


def _const_spec(shape):
    return pl.BlockSpec(shape, lambda i: (0,) * len(shape), pipeline_mode=pl.Buffered(1))


def _vmem_limit(nbytes):
    return int(min(nbytes * 1.25 + (8 << 20), V7X_VMEM_BYTES - (4 << 20)))


def _ffn_kernel(x_ref, g_ref, wg_ref, wu_ref, wd_ref, *rest, cols, final):
    if final:
        fg_ref, o_ref, h_scr, act_scr = rest
    else:
        o_ref, h_scr, act_scr = rest
    d_ff = wg_ref.shape[1]
    h_scr[...] = _rmsnorm(x_ref[...], g_ref[...]).astype(_BF16)
    for j in range(d_ff // cols):
        sl = slice(j * cols, (j + 1) * cols)
        h = h_scr[...]
        gate = jnp.dot(h, wg_ref[:, sl], preferred_element_type=_F32)
        up = jnp.dot(h, wu_ref[:, sl], preferred_element_type=_F32)
        act_scr[:, sl] = (gate * jax.nn.sigmoid(gate) * up).astype(_BF16)
    down = jnp.dot(act_scr[...], wd_ref[...], preferred_element_type=_F32)
    y = x_ref[...] + 0.5 * down
    if final:
        y = _rmsnorm(y, fg_ref[...])
    o_ref[...] = y


def _ffn_call(x, g, wg, wu, wd, final_g=None):
    s, d = x.shape
    d_ff = wg.shape[1]
    rows, cols = FFN_ROWS, FFN_COLS
    assert s % rows == 0 and d_ff % cols == 0
    final = final_g is not None
    row_spec = pl.BlockSpec((rows, d), lambda i: (i, 0))
    in_specs = [row_spec, _const_spec((1, d)), _const_spec((d, d_ff)), _const_spec((d, d_ff)),
                _const_spec((d_ff, d))]
    args = [x, g, wg, wu, wd]
    if final:
        in_specs.append(_const_spec((1, d)))
        args.append(final_g)
    vmem = 3 * d * d_ff * 2 + 4 * rows * d * 4 + rows * d * 2 + rows * d_ff * 2 + 4 * rows * cols * 4
    return pl.pallas_call(
        functools.partial(_ffn_kernel, cols=cols, final=final),
        grid=(s // rows,),
        in_specs=in_specs,
        out_specs=row_spec,
        out_shape=jax.ShapeDtypeStruct((s, d), _F32),
        scratch_shapes=[pltpu.VMEM((rows, d), _BF16), pltpu.VMEM((rows, d_ff), _BF16)],
        compiler_params=pltpu.CompilerParams(dimension_semantics=("arbitrary",),
                                             vmem_limit_bytes=_vmem_limit(vmem)),
        name="swiglu_final" if final else "swiglu",
    )(*args)


def _gate_terms(gates, m_in, chunk):
    rows = gates.shape[0]
    g8 = gates.T[0:2 * N_HEADS, :]
    ig = g8[0:N_HEADS, :]
    t_in_chunk = lax.broadcasted_iota(jnp.int32, ig.shape, 1) % chunk
    b = jax.nn.log_sigmoid(g8[N_HEADS:2 * N_HEADS, :])
    shift = 1
    while shift < chunk:
        b = b + jnp.where(t_in_chunk >= shift, pltpu.roll(b, shift, axis=1), 0.0)
        shift *= 2
    drow = ig - b
    cm = drow
    shift = 1
    while shift < chunk:
        cm = jnp.maximum(cm, jnp.where(t_in_chunk >= shift, pltpu.roll(cm, shift, axis=1), -jnp.inf))
        shift *= 2
    neg_mx, inter_w, exp_neg_m, w_state, decays = [], [], [], [], []
    m_prev = m_in
    for c in range(rows // chunk):
        sl = slice(c * chunk, (c + 1) * chunk)
        g = b[:, (c + 1) * chunk - 1:(c + 1) * chunk]
        a = g + drow[:, sl]
        m_new = jnp.maximum(g + m_prev, jnp.max(a, axis=1, keepdims=True))
        mx = jnp.maximum(m_prev, cm[:, sl])
        neg_mx.append(-mx)
        inter_w.append(jnp.exp(m_prev - mx))
        exp_neg_m.append(jnp.exp(-(b[:, sl] + mx)))
        w_state.append(jnp.exp(a - m_new))
        decays.append(jnp.exp(g + m_prev - m_new))
        m_prev = m_new
    cat = lambda parts: jnp.concatenate(parts, axis=1)
    stacked = jnp.concatenate(
        [cat(neg_mx), cat(inter_w), cat(exp_neg_m), cat(w_state), jnp.zeros((LANES - 4 * N_HEADS, rows), _F32)],
        axis=0)
    return stacked.T, drow, decays, m_prev


def _mlstm_chunk(q, k, v, neg_mx, inter_w, exp_neg_m, w_state, d_row, decay, c_prev, n_prev):
    length = q.shape[0]
    qb, kb, vb = q.astype(_BF16), k.astype(_BF16), v.astype(_BF16)
    row_id = lax.broadcasted_iota(jnp.int32, (length, length), 0)
    col_id = lax.broadcasted_iota(jnp.int32, (length, length), 1)
    dw = jnp.exp(jnp.where(row_id >= col_id, neg_mx + d_row, -jnp.inf))
    scores = lax.dot_general(qb, kb, (((1,), (1,)), ((), ())), preferred_element_type=_F32) * dw
    num = (jnp.dot(scores.astype(_BF16), vb, preferred_element_type=_F32)
           + inter_w * jnp.dot(qb, c_prev.astype(_BF16), preferred_element_type=_F32))
    qn = (jnp.sum(scores, axis=1, keepdims=True)
          + inter_w * jnp.sum(q * n_prev, axis=1, keepdims=True))
    h = num / jnp.maximum(jnp.abs(qn), exp_neg_m)
    kw = k * w_state
    c_new = decay * c_prev + lax.dot_general(kw.astype(_BF16), vb, (((0,), (0,)), ((), ())),
                                             preferred_element_type=_F32)
    n_new = decay * n_prev + jnp.sum(kw, axis=0, keepdims=True)
    return h, c_new, n_new


def _project(x, g_ref, win_ref, proj_ref):
    hb = _rmsnorm(x, g_ref[...]).astype(_BF16)
    proj_ref[HALO:, :] = jnp.dot(hb, win_ref[...], preferred_element_type=_F32)


def _mix_tile(proj_ref, x_ref, o_ref, row0, first_pos, *, bgate_ref, convw_ref, convb_ref, mhn_ref, poolw_ref,
              pscale_ref, wout_ref, qk_scr, mix_scr, c_scr, n_scr, m_scr, rows, chunk):
    width = N_HEADS * HEAD_DIM
    off_pool = 4 * width
    off_gate = off_pool + len(POOL_WINDOWS) * POOL_GROUP_DIM

    for rb in range(rows // CONV_BLOCK):
        base = HALO + rb * CONV_BLOCK
        conv = convb_ref[...] + convw_ref[CONV_WIDTH - 1:CONV_WIDTH, :] * proj_ref[base:base + CONV_BLOCK, 0:2 * width]
        for j in range(CONV_WIDTH - 1):
            off = base - (CONV_WIDTH - 1) + j
            conv = conv + convw_ref[j:j + 1, :] * proj_ref[off:off + CONV_BLOCK, 0:2 * width]
        qk_scr[rb * CONV_BLOCK:(rb + 1) * CONV_BLOCK, :] = conv * jax.nn.sigmoid(conv)

    gates = proj_ref[HALO:, off_gate:off_gate + LANES] + bgate_ref[...]
    cols, drow, decays, m_out = _gate_terms(gates, m_scr[0:N_HEADS, 0:1], chunk)
    m_scr[0:N_HEADS, 0:1] = m_out

    for c in range(rows // chunk):
        rs = slice(c * chunk, (c + 1) * chunk)
        rh = slice(HALO + c * chunk, HALO + (c + 1) * chunk)
        for hd in range(N_HEADS):
            hs = slice(hd * HEAD_DIM, (hd + 1) * HEAD_DIM)
            q = qk_scr[rs, hs]
            k = qk_scr[rs, width + hd * HEAD_DIM: width + (hd + 1) * HEAD_DIM] * (HEAD_DIM ** -0.5)
            v = proj_ref[rh, 2 * width + hd * HEAD_DIM: 2 * width + (hd + 1) * HEAD_DIM]
            og = proj_ref[rh, 3 * width + hd * HEAD_DIM: 3 * width + (hd + 1) * HEAD_DIM]
            h, c_new, n_new = _mlstm_chunk(
                q, k, v,
                cols[rs, hd:hd + 1], cols[rs, N_HEADS + hd:N_HEADS + hd + 1],
                cols[rs, 2 * N_HEADS + hd:2 * N_HEADS + hd + 1], cols[rs, 3 * N_HEADS + hd:3 * N_HEADS + hd + 1],
                drow[hd:hd + 1, rs], decays[c][hd:hd + 1, :], c_scr[hd], n_scr[hd])
            c_scr[hd] = c_new
            n_scr[hd] = n_new
            h = h * lax.rsqrt(jnp.mean(h * h, axis=-1, keepdims=True) + EPS)
            mix_scr[rs, hs] = (jax.nn.sigmoid(og) * (h * mhn_ref[:, hs])).astype(_BF16)

    for rb in range(rows // POOL_BLOCK):
        r0 = rb * POOL_BLOCK
        pos = first_pos + r0 + lax.broadcasted_iota(jnp.int32, (POOL_BLOCK, 1), 0) + 1
        for gi, win in enumerate(POOL_WINDOWS):
            cs = slice(off_pool + gi * POOL_GROUP_DIM, off_pool + (gi + 1) * POOL_GROUP_DIM)
            total = proj_ref[r0:r0 + HALO + POOL_BLOCK, cs]
            span = 1
            while span < win:
                total = total + pltpu.roll(total, span, axis=0)
                span *= 2
            inv_count = 1.0 / jnp.minimum(pos, win).astype(_F32)
            pooled = total[HALO:, :] * inv_count - proj_ref[HALO + r0:HALO + r0 + POOL_BLOCK, cs]
            mixed = jnp.dot(pooled.astype(_BF16), poolw_ref[gi], preferred_element_type=_F32)
            mix_scr[r0:r0 + POOL_BLOCK, width + gi * POOL_GROUP_DIM: width + (gi + 1) * POOL_GROUP_DIM] = (
                mixed * pscale_ref[:, gi * POOL_GROUP_DIM:(gi + 1) * POOL_GROUP_DIM]).astype(_BF16)

    o_ref[row0:row0 + rows, :] = x_ref[row0:row0 + rows, :] + jnp.dot(
        mix_scr[...], wout_ref[...], preferred_element_type=_F32)


def _mixer_kernel(x_ref, xn_ref, g_ref, win_ref, bgate_ref, convw_ref, convb_ref, mhn_ref, poolw_ref,
                  pscale_ref, wout_ref, o_ref, proj_a, proj_b, qk_scr, mix_scr, c_scr, n_scr, m_scr, *, chunk):
    rows = xn_ref.shape[0]
    step = pl.program_id(0)
    tile = functools.partial(
        _mix_tile, bgate_ref=bgate_ref, convw_ref=convw_ref, convb_ref=convb_ref, mhn_ref=mhn_ref,
        poolw_ref=poolw_ref, pscale_ref=pscale_ref, wout_ref=wout_ref, qk_scr=qk_scr, mix_scr=mix_scr,
        c_scr=c_scr, n_scr=n_scr, m_scr=m_scr, rows=rows, chunk=chunk)

    @pl.when(step == 0)
    def _():
        proj_a[0:HALO, :] = jnp.zeros((HALO, proj_a.shape[1]), _F32)
        c_scr[...] = jnp.zeros(c_scr.shape, _F32)
        n_scr[...] = jnp.zeros(n_scr.shape, _F32)
        m_scr[...] = jnp.zeros(m_scr.shape, _F32)
        _project(x_ref[0:rows, :], g_ref, win_ref, proj_a)

    _project(x_ref[rows:2 * rows, :], g_ref, win_ref, proj_b)
    proj_b[0:HALO, :] = proj_a[rows:rows + HALO, :]
    tile(proj_a, x_ref, o_ref, 0, step * (2 * rows))
    _project(xn_ref[...], g_ref, win_ref, proj_a)
    proj_a[0:HALO, :] = proj_b[rows:rows + HALO, :]
    tile(proj_b, x_ref, o_ref, rows, step * (2 * rows) + rows)


def _mixer_call(x, g, win, bgate, convw, convb, mhn, poolw, pscale, wout):
    s, d = x.shape
    rows, chunk = MIX_ROWS, MIX_CHUNK
    width = N_HEADS * HEAD_DIM
    cols = win.shape[1]
    assert s % (2 * rows) == 0 and rows % chunk == 0 and chunk % LANES == 0
    assert rows % CONV_BLOCK == 0 and rows % POOL_BLOCK == 0
    n_steps = s // (2 * rows)
    pair_spec = pl.BlockSpec((2 * rows, d), lambda i: (i, 0))
    next_spec = pl.BlockSpec((rows, d), lambda i: (jnp.minimum(2 * i + 2, 2 * n_steps - 1), 0))
    consts = [g, win, bgate, convw, convb, mhn, poolw, pscale, wout]
    in_specs = [pair_spec, next_spec] + [_const_spec(a.shape) for a in consts]
    vmem = (sum(a.size * a.dtype.itemsize for a in consts) + 10 * rows * d * 4
            + 2 * (rows + HALO) * cols * 4 + rows * 2 * width * 6 + 4 * rows * cols * 4)
    return pl.pallas_call(
        functools.partial(_mixer_kernel, chunk=chunk),
        grid=(n_steps,),
        in_specs=in_specs,
        out_specs=pair_spec,
        out_shape=jax.ShapeDtypeStruct((s, d), _F32),
        scratch_shapes=[
            pltpu.VMEM((rows + HALO, cols), _F32),
            pltpu.VMEM((rows + HALO, cols), _F32),
            pltpu.VMEM((rows, 2 * width), _F32),
            pltpu.VMEM((rows, 2 * width), _BF16),
            pltpu.VMEM((N_HEADS, HEAD_DIM, HEAD_DIM), _F32),
            pltpu.VMEM((N_HEADS, 1, HEAD_DIM), _F32),
            pltpu.VMEM((8, LANES), _F32),
        ],
        compiler_params=pltpu.CompilerParams(dimension_semantics=("arbitrary",),
                                             vmem_limit_bytes=_vmem_limit(vmem)),
        name="mixer",
    )(x, x, *consts)


def kernel(x, ffn1_norm, ffn1_w_gate, ffn1_w_up, ffn1_w_down, mix_norm, w_in, b_gates, conv_w, conv_b,
           mh_norm, pool_w, pool_scale, w_out, ffn2_norm, ffn2_w_gate, ffn2_w_up, ffn2_w_down, final_norm):
    bsz, s, d = x.shape
    depth = ffn1_norm.shape[0]
    assert depth >= 1
    width = N_HEADS * HEAD_DIM
    off_g = 4 * width
    off_p = off_g + 2 * N_HEADS
    bf = lambda a: a.astype(_BF16)
    outs = []
    for b in range(bsz):
        xb = x[b]
        for l in range(depth):
            xb = _ffn_call(xb, ffn1_norm[l][None, :], bf(ffn1_w_gate[l]), bf(ffn1_w_up[l]), bf(ffn1_w_down[l]))
            win = jnp.concatenate(
                [w_in[l][:, :off_g], w_in[l][:, off_p:],
                 jnp.pad(w_in[l][:, off_g:off_p], ((0, 0), (0, LANES - 2 * N_HEADS)))], axis=1)
            bgate = jnp.pad(b_gates[l], (0, LANES - 2 * N_HEADS))[None, :]
            xb = _mixer_call(
                xb, mix_norm[l][None, :], bf(win), bgate, conv_w[l], conv_b[l][None, :], mh_norm[l][None, :],
                bf(pool_w[l]), pool_scale[l][None, :], bf(w_out[l]))
            last = l == depth - 1
            xb = _ffn_call(xb, ffn2_norm[l][None, :], bf(ffn2_w_gate[l]), bf(ffn2_w_up[l]), bf(ffn2_w_down[l]),
                           final_g=final_norm[None, :] if last else None)
        outs.append(xb)
    return jnp.stack(outs, axis=0)
```

```python
import functools

import jax
import jax.numpy as jnp
from jax import lax
from jax.experimental import pallas as pl
from jax.experimental.pallas import tpu as pltpu

EPS = 1e-6
N_HEADS = 4
HEAD_DIM = 128
POOL_WINDOWS = (2, 4, 8, 16)
POOL_GROUP_DIM = 128
CONV_WIDTH = 4

LANES = 128
HALO = 16
V7X_VMEM_BYTES = 64 * 1024 * 1024

FFN_ROWS = 512
FFN_COLS = 256
WEIGHT_STAGE_SPLIT = 4
MIX_ROWS = 512
MIX_CHUNK = 256
CONV_BLOCK = 32
POOL_BLOCK = 128

_F32 = jnp.float32
_BF16 = jnp.bfloat16


def _rmsnorm(x, g):
    y = x * lax.rsqrt(jnp.mean(x * x, axis=-1, keepdims=True) + EPS)
    return y * g


def _const_spec(shape):
    return pl.BlockSpec(shape, lambda i: (0,) * len(shape), pipeline_mode=pl.Buffered(1))


def _vmem_limit(nbytes):
    return int(min(nbytes * 1.25 + (8 << 20), V7X_VMEM_BYTES - (4 << 20)))


def _stage_weights(jobs):
    copies = [pltpu.make_async_copy(src, stage, sem) for src, stage, sem, _ in jobs]
    copies[0].start()
    for i, (_, stage, _, dst) in enumerate(jobs):
        if i + 1 < len(jobs):
            copies[i + 1].start()
        copies[i].wait()
        dst[...] = stage[...].astype(_BF16)


def _ffn_kernel(x_ref, g_ref, wg_hbm, wu_hbm, wd_hbm, *rest, cols, final):
    if final:
        fg_ref, o_ref, wg_ref, wu_ref, wd_ref, stage_in, stage_out, sems, h_scr, act_scr = rest
    else:
        o_ref, wg_ref, wu_ref, wd_ref, stage_in, stage_out, sems, h_scr, act_scr = rest
    d, d_ff = wg_ref.shape

    @pl.when(pl.program_id(0) == 0)
    def _():
        jobs = []
        for w_hbm, w_ref, stage, sem0 in ((wg_hbm, wg_ref, stage_in, 0), (wu_hbm, wu_ref, stage_in, 0),
                                          (wd_hbm, wd_ref, stage_out, 2)):
            chunk = stage.shape[1]
            for c in range(w_hbm.shape[0] // chunk):
                rows = pl.ds(c * chunk, chunk)
                slot = len(jobs) % 2
                jobs.append((w_hbm.at[rows], stage.at[slot], sems.at[sem0 + slot], w_ref.at[rows]))
        _stage_weights(jobs)

    h_scr[...] = _rmsnorm(x_ref[...], g_ref[...]).astype(_BF16)
    for j in range(d_ff // cols):
        sl = slice(j * cols, (j + 1) * cols)
        h = h_scr[...]
        gate = jnp.dot(h, wg_ref[:, sl], preferred_element_type=_F32)
        up = jnp.dot(h, wu_ref[:, sl], preferred_element_type=_F32)
        act_scr[:, sl] = (gate * jax.nn.sigmoid(gate) * up).astype(_BF16)
    down = jnp.dot(act_scr[...], wd_ref[...], preferred_element_type=_F32)
    y = x_ref[...] + 0.5 * down
    if final:
        y = _rmsnorm(y, fg_ref[...])
    o_ref[...] = y


def _ffn_call(x, g, wg, wu, wd, final_g=None):
    s, d = x.shape
    d_ff = wg.shape[1]
    rows, cols = FFN_ROWS, FFN_COLS
    in_chunk, out_chunk = d // WEIGHT_STAGE_SPLIT, d_ff // WEIGHT_STAGE_SPLIT
    assert s % rows == 0 and d_ff % cols == 0
    assert d % in_chunk == 0 and d_ff % out_chunk == 0 and in_chunk % 16 == 0 and out_chunk % 16 == 0
    assert WEIGHT_STAGE_SPLIT % 2 == 0
    final = final_g is not None
    row_spec = pl.BlockSpec((rows, d), lambda i: (i, 0))
    hbm_spec = pl.BlockSpec(memory_space=pl.ANY)
    in_specs = [row_spec, _const_spec((1, d)), hbm_spec, hbm_spec, hbm_spec]
    args = [x, g, wg, wu, wd]
    if final:
        in_specs.append(_const_spec((1, d)))
        args.append(final_g)
    vmem = (3 * d * d_ff * 2 + 2 * (in_chunk * d_ff + out_chunk * d) * 4 + 4 * rows * d * 4 + rows * d * 2
            + rows * d_ff * 2 + 4 * rows * cols * 4)
    return pl.pallas_call(
        functools.partial(_ffn_kernel, cols=cols, final=final),
        grid=(s // rows,),
        in_specs=in_specs,
        out_specs=row_spec,
        out_shape=jax.ShapeDtypeStruct((s, d), _F32),
        scratch_shapes=[
            pltpu.VMEM((d, d_ff), _BF16), pltpu.VMEM((d, d_ff), _BF16), pltpu.VMEM((d_ff, d), _BF16),
            pltpu.VMEM((2, in_chunk, d_ff), _F32),
            pltpu.VMEM((2, out_chunk, d), _F32),
            pltpu.SemaphoreType.DMA((4,)),
            pltpu.VMEM((rows, d), _BF16), pltpu.VMEM((rows, d_ff), _BF16)],
        compiler_params=pltpu.CompilerParams(dimension_semantics=("arbitrary",),
                                             vmem_limit_bytes=_vmem_limit(vmem)),
        name="swiglu_final" if final else "swiglu",
    )(*args)


def _gate_terms(gates, m_in, chunk):
    rows = gates.shape[0]
    g8 = gates.T[0:2 * N_HEADS, :]
    ig = g8[0:N_HEADS, :]
    t_in_chunk = lax.broadcasted_iota(jnp.int32, ig.shape, 1) % chunk
    b = jax.nn.log_sigmoid(g8[N_HEADS:2 * N_HEADS, :])
    shift = 1
    while shift < chunk:
        b = b + jnp.where(t_in_chunk >= shift, pltpu.roll(b, shift, axis=1), 0.0)
        shift *= 2
    drow = ig - b
    cm = drow
    shift = 1
    while shift < chunk:
        cm = jnp.maximum(cm, jnp.where(t_in_chunk >= shift, pltpu.roll(cm, shift, axis=1), -jnp.inf))
        shift *= 2
    neg_mx, inter_w, exp_neg_m, w_state, decays = [], [], [], [], []
    m_prev = m_in
    for c in range(rows // chunk):
        sl = slice(c * chunk, (c + 1) * chunk)
        g = b[:, (c + 1) * chunk - 1:(c + 1) * chunk]
        a = g + drow[:, sl]
        m_new = jnp.maximum(g + m_prev, jnp.max(a, axis=1, keepdims=True))
        mx = jnp.maximum(m_prev, cm[:, sl])
        neg_mx.append(-mx)
        inter_w.append(jnp.exp(m_prev - mx))
        exp_neg_m.append(jnp.exp(-(b[:, sl] + mx)))
        w_state.append(jnp.exp(a - m_new))
        decays.append(jnp.exp(g + m_prev - m_new))
        m_prev = m_new
    cat = lambda parts: jnp.concatenate(parts, axis=1)
    stacked = jnp.concatenate(
        [cat(neg_mx), cat(inter_w), cat(exp_neg_m), cat(w_state), jnp.zeros((LANES - 4 * N_HEADS, rows), _F32)],
        axis=0)
    return stacked.T, drow, decays, m_prev


def _mlstm_chunk(q, k, v, neg_mx, inter_w, exp_neg_m, w_state, d_row, decay, c_prev, n_prev):
    length = q.shape[0]
    qb, kb, vb = q.astype(_BF16), k.astype(_BF16), v.astype(_BF16)
    row_id = lax.broadcasted_iota(jnp.int32, (length, length), 0)
    col_id = lax.broadcasted_iota(jnp.int32, (length, length), 1)
    dw = jnp.exp(jnp.where(row_id >= col_id, neg_mx + d_row, -jnp.inf))
    scores = lax.dot_general(qb, kb, (((1,), (1,)), ((), ())), preferred_element_type=_F32) * dw
    num = (jnp.dot(scores.astype(_BF16), vb, preferred_element_type=_F32)
           + inter_w * jnp.dot(qb, c_prev.astype(_BF16), preferred_element_type=_F32))
    qn = (jnp.sum(scores, axis=1, keepdims=True)
          + inter_w * jnp.sum(q * n_prev, axis=1, keepdims=True))
    h = num / jnp.maximum(jnp.abs(qn), exp_neg_m)
    kw = k * w_state
    c_new = decay * c_prev + lax.dot_general(kw.astype(_BF16), vb, (((0,), (0,)), ((), ())),
                                             preferred_element_type=_F32)
    n_new = decay * n_prev + jnp.sum(kw, axis=0, keepdims=True)
    return h, c_new, n_new


def _project(x, g_ref, win_ref, proj_ref):
    hb = _rmsnorm(x, g_ref[...]).astype(_BF16)
    proj_ref[HALO:, :] = jnp.dot(hb, win_ref[...], preferred_element_type=_F32)


def _mix_tile(proj_ref, x_ref, o_ref, row0, first_pos, *, bgate_ref, convw_ref, convb_ref, mhn_ref, poolw_ref,
              pscale_ref, wout_ref, qk_scr, mix_scr, c_scr, n_scr, m_scr, rows, chunk):
    width = N_HEADS * HEAD_DIM
    off_pool = 4 * width
    off_gate = off_pool + len(POOL_WINDOWS) * POOL_GROUP_DIM

    for rb in range(rows // CONV_BLOCK):
        base = HALO + rb * CONV_BLOCK
        conv = convb_ref[...] + convw_ref[CONV_WIDTH - 1:CONV_WIDTH, :] * proj_ref[base:base + CONV_BLOCK, 0:2 * width]
        for j in range(CONV_WIDTH - 1):
            off = base - (CONV_WIDTH - 1) + j
            conv = conv + convw_ref[j:j + 1, :] * proj_ref[off:off + CONV_BLOCK, 0:2 * width]
        qk_scr[rb * CONV_BLOCK:(rb + 1) * CONV_BLOCK, :] = conv * jax.nn.sigmoid(conv)

    gates = proj_ref[HALO:, off_gate:off_gate + LANES] + bgate_ref[...]
    cols, drow, decays, m_out = _gate_terms(gates, m_scr[0:N_HEADS, 0:1], chunk)
    m_scr[0:N_HEADS, 0:1] = m_out

    for c in range(rows // chunk):
        rs = slice(c * chunk, (c + 1) * chunk)
        rh = slice(HALO + c * chunk, HALO + (c + 1) * chunk)
        for hd in range(N_HEADS):
            hs = slice(hd * HEAD_DIM, (hd + 1) * HEAD_DIM)
            q = qk_scr[rs, hs]
            k = qk_scr[rs, width + hd * HEAD_DIM: width + (hd + 1) * HEAD_DIM] * (HEAD_DIM ** -0.5)
            v = proj_ref[rh, 2 * width + hd * HEAD_DIM: 2 * width + (hd + 1) * HEAD_DIM]
            og = proj_ref[rh, 3 * width + hd * HEAD_DIM: 3 * width + (hd + 1) * HEAD_DIM]
            h, c_new, n_new = _mlstm_chunk(
                q, k, v,
                cols[rs, hd:hd + 1], cols[rs, N_HEADS + hd:N_HEADS + hd + 1],
                cols[rs, 2 * N_HEADS + hd:2 * N_HEADS + hd + 1], cols[rs, 3 * N_HEADS + hd:3 * N_HEADS + hd + 1],
                drow[hd:hd + 1, rs], decays[c][hd:hd + 1, :], c_scr[hd], n_scr[hd])
            c_scr[hd] = c_new
            n_scr[hd] = n_new
            h = h * lax.rsqrt(jnp.mean(h * h, axis=-1, keepdims=True) + EPS)
            mix_scr[rs, hs] = (jax.nn.sigmoid(og) * (h * mhn_ref[:, hs])).astype(_BF16)

    for rb in range(rows // POOL_BLOCK):
        r0 = rb * POOL_BLOCK
        pos = first_pos + r0 + lax.broadcasted_iota(jnp.int32, (POOL_BLOCK, 1), 0) + 1
        for gi, win in enumerate(POOL_WINDOWS):
            cs = slice(off_pool + gi * POOL_GROUP_DIM, off_pool + (gi + 1) * POOL_GROUP_DIM)
            total = proj_ref[r0:r0 + HALO + POOL_BLOCK, cs]
            span = 1
            while span < win:
                total = total + pltpu.roll(total, span, axis=0)
                span *= 2
            inv_count = 1.0 / jnp.minimum(pos, win).astype(_F32)
            pooled = total[HALO:, :] * inv_count - proj_ref[HALO + r0:HALO + r0 + POOL_BLOCK, cs]
            mixed = jnp.dot(pooled.astype(_BF16), poolw_ref[gi], preferred_element_type=_F32)
            mix_scr[r0:r0 + POOL_BLOCK, width + gi * POOL_GROUP_DIM: width + (gi + 1) * POOL_GROUP_DIM] = (
                mixed * pscale_ref[:, gi * POOL_GROUP_DIM:(gi + 1) * POOL_GROUP_DIM]).astype(_BF16)

    o_ref[row0:row0 + rows, :] = x_ref[row0:row0 + rows, :] + jnp.dot(
        mix_scr[...], wout_ref[...], preferred_element_type=_F32)


def _mixer_kernel(x_ref, xn_ref, g_ref, win_ref, bgate_ref, convw_ref, convb_ref, mhn_ref, poolw_ref,
                  pscale_ref, wout_ref, o_ref, proj_a, proj_b, qk_scr, mix_scr, c_scr, n_scr, m_scr, *, chunk):
    rows = xn_ref.shape[0]
    step = pl.program_id(0)
    tile = functools.partial(
        _mix_tile, bgate_ref=bgate_ref, convw_ref=convw_ref, convb_ref=convb_ref, mhn_ref=mhn_ref,
        poolw_ref=poolw_ref, pscale_ref=pscale_ref, wout_ref=wout_ref, qk_scr=qk_scr, mix_scr=mix_scr,
        c_scr=c_scr, n_scr=n_scr, m_scr=m_scr, rows=rows, chunk=chunk)

    @pl.when(step == 0)
    def _():
        proj_a[0:HALO, :] = jnp.zeros((HALO, proj_a.shape[1]), _F32)
        c_scr[...] = jnp.zeros(c_scr.shape, _F32)
        n_scr[...] = jnp.zeros(n_scr.shape, _F32)
        m_scr[...] = jnp.zeros(m_scr.shape, _F32)
        _project(x_ref[0:rows, :], g_ref, win_ref, proj_a)

    _project(x_ref[rows:2 * rows, :], g_ref, win_ref, proj_b)
    proj_b[0:HALO, :] = proj_a[rows:rows + HALO, :]
    tile(proj_a, x_ref, o_ref, 0, step * (2 * rows))
    _project(xn_ref[...], g_ref, win_ref, proj_a)
    proj_a[0:HALO, :] = proj_b[rows:rows + HALO, :]
    tile(proj_b, x_ref, o_ref, rows, step * (2 * rows) + rows)


def _mixer_call(x, g, win, bgate, convw, convb, mhn, poolw, pscale, wout):
    s, d = x.shape
    rows, chunk = MIX_ROWS, MIX_CHUNK
    width = N_HEADS * HEAD_DIM
    cols = win.shape[1]
    assert s % (2 * rows) == 0 and rows % chunk == 0 and chunk % LANES == 0
    assert rows % CONV_BLOCK == 0 and rows % POOL_BLOCK == 0
    n_steps = s // (2 * rows)
    pair_spec = pl.BlockSpec((2 * rows, d), lambda i: (i, 0))
    next_spec = pl.BlockSpec((rows, d), lambda i: (jnp.minimum(2 * i + 2, 2 * n_steps - 1), 0))
    consts = [g, win, bgate, convw, convb, mhn, poolw, pscale, wout]
    in_specs = [pair_spec, next_spec] + [_const_spec(a.shape) for a in consts]
    vmem = (sum(a.size * a.dtype.itemsize for a in consts) + 10 * rows * d * 4
            + 2 * (rows + HALO) * cols * 4 + rows * 2 * width * 6 + 4 * rows * cols * 4)
    return pl.pallas_call(
        functools.partial(_mixer_kernel, chunk=chunk),
        grid=(n_steps,),
        in_specs=in_specs,
        out_specs=pair_spec,
        out_shape=jax.ShapeDtypeStruct((s, d), _F32),
        scratch_shapes=[
            pltpu.VMEM((rows + HALO, cols), _F32),
            pltpu.VMEM((rows + HALO, cols), _F32),
            pltpu.VMEM((rows, 2 * width), _F32),
            pltpu.VMEM((rows, 2 * width), _BF16),
            pltpu.VMEM((N_HEADS, HEAD_DIM, HEAD_DIM), _F32),
            pltpu.VMEM((N_HEADS, 1, HEAD_DIM), _F32),
            pltpu.VMEM((8, LANES), _F32),
        ],
        compiler_params=pltpu.CompilerParams(dimension_semantics=("arbitrary",),
                                             vmem_limit_bytes=_vmem_limit(vmem)),
        name="mixer",
    )(x, x, *consts)


def kernel(x, ffn1_norm, ffn1_w_gate, ffn1_w_up, ffn1_w_down, mix_norm, w_in, b_gates, conv_w, conv_b,
           mh_norm, pool_w, pool_scale, w_out, ffn2_norm, ffn2_w_gate, ffn2_w_up, ffn2_w_down, final_norm):
    bsz, s, d = x.shape
    depth = ffn1_norm.shape[0]
    assert depth >= 1
    width = N_HEADS * HEAD_DIM
    off_g = 4 * width
    off_p = off_g + 2 * N_HEADS
    bf = lambda a: a.astype(_BF16)
    outs = []
    for b in range(bsz):
        xb = x[b]
        for l in range(depth):
            xb = _ffn_call(xb, ffn1_norm[l][None, :], ffn1_w_gate[l], ffn1_w_up[l], ffn1_w_down[l])
            win = jnp.concatenate(
                [w_in[l][:, :off_g], w_in[l][:, off_p:],
                 jnp.pad(w_in[l][:, off_g:off_p], ((0, 0), (0, LANES - 2 * N_HEADS)))], axis=1)
            bgate = jnp.pad(b_gates[l], (0, LANES - 2 * N_HEADS))[None, :]
            xb = _mixer_call(
                xb, mix_norm[l][None, :], bf(win), bgate, conv_w[l], conv_b[l][None, :], mh_norm[l][None, :],
                bf(pool_w[l]), pool_scale[l][None, :], bf(w_out[l]))
            last = l == depth - 1
            xb = _ffn_call(xb, ffn2_norm[l][None, :], ffn2_w_gate[l], ffn2_w_up[l], ffn2_w_down[l],
                           final_g=final_norm[None, :] if last else None)
        outs.append(xb)
    return jnp.stack(outs, axis=0)
```

```python
import functools

import jax
import jax.numpy as jnp
from jax import lax
from jax.experimental import pallas as pl
from jax.experimental.pallas import tpu as pltpu

EPS = 1e-6
N_HEADS = 4
HEAD_DIM = 128
POOL_WINDOWS = (2, 4, 8, 16)
POOL_GROUP_DIM = 128
CONV_WIDTH = 4

LANES = 128
HALO = 16
V7X_VMEM_BYTES = 64 * 1024 * 1024

FFN_ROWS = 512
FFN_SUBTILES = 2
FFN_COLS = 256
WEIGHT_STAGE_SPLIT = 4
MIX_ROWS = 512
MIX_CHUNK = 256
CONV_BLOCK = 32
POOL_BLOCK = 128

_F32 = jnp.float32
_BF16 = jnp.bfloat16


def _rmsnorm(x, g):
    y = x * lax.rsqrt(jnp.mean(x * x, axis=-1, keepdims=True) + EPS)
    return y * g


def _const_spec(shape):
    return pl.BlockSpec(shape, lambda i: (0,) * len(shape), pipeline_mode=pl.Buffered(1))


def _vmem_limit(nbytes):
    return int(min(nbytes * 1.25 + (8 << 20), V7X_VMEM_BYTES - (4 << 20)))


def _stage_weights(jobs):
    copies = [pltpu.make_async_copy(src, stage, sem) for src, stage, sem, _ in jobs]
    copies[0].start()
    for i, (_, stage, _, dst) in enumerate(jobs):
        if i + 1 < len(jobs):
            copies[i + 1].start()
        copies[i].wait()
        dst[...] = stage[...].astype(_BF16)


def _ffn_kernel(x_ref, g_ref, wg_hbm, wu_hbm, wd_hbm, *rest, cols, final):
    if final:
        fg_ref, o_ref, h_scr, act_scr, wg_ref, wu_ref, wd_ref, stage_in, stage_out, sems = rest
    else:
        o_ref, h_scr, act_scr, wg_ref, wu_ref, wd_ref, stage_in, stage_out, sems = rest
    d, d_ff = wg_ref.shape

    @pl.when(pl.program_id(0) == 0)
    def _():
        jobs = []
        for w_hbm, w_ref, stage, sem0 in ((wg_hbm, wg_ref, stage_in, 0), (wu_hbm, wu_ref, stage_in, 0),
                                          (wd_hbm, wd_ref, stage_out, 2)):
            chunk = stage.shape[1]
            for c in range(w_hbm.shape[0] // chunk):
                rows = pl.ds(c * chunk, chunk)
                slot = len(jobs) % 2
                jobs.append((w_hbm.at[rows], stage.at[slot], sems.at[sem0 + slot], w_ref.at[rows]))
        _stage_weights(jobs)

    n_sub, rows = h_scr.shape[0], h_scr.shape[1]
    for t in range(n_sub):
        h_scr[t] = _rmsnorm(x_ref[t * rows:(t + 1) * rows, :], g_ref[...]).astype(_BF16)
    for t in range(n_sub):
        for j in range(d_ff // cols):
            sl = slice(j * cols, (j + 1) * cols)
            h = h_scr[t]
            gate = jnp.dot(h, wg_ref[:, sl], preferred_element_type=_F32)
            up = jnp.dot(h, wu_ref[:, sl], preferred_element_type=_F32)
            act_scr[t, :, sl] = (gate * jax.nn.sigmoid(gate) * up).astype(_BF16)
        down = jnp.dot(act_scr[t], wd_ref[...], preferred_element_type=_F32)
        y = x_ref[t * rows:(t + 1) * rows, :] + 0.5 * down
        if final:
            y = _rmsnorm(y, fg_ref[...])
        o_ref[t * rows:(t + 1) * rows, :] = y


def _ffn_call(x, g, wg, wu, wd, final_g=None):
    s, d = x.shape
    d_ff = wg.shape[1]
    rows, cols, n_sub = FFN_ROWS, FFN_COLS, FFN_SUBTILES
    block = n_sub * rows
    in_chunk, out_chunk = d // WEIGHT_STAGE_SPLIT, d_ff // WEIGHT_STAGE_SPLIT
    assert s % block == 0 and d_ff % cols == 0
    assert d % in_chunk == 0 and d_ff % out_chunk == 0 and in_chunk % 16 == 0 and out_chunk % 16 == 0
    assert WEIGHT_STAGE_SPLIT % 2 == 0
    final = final_g is not None
    row_spec = pl.BlockSpec((block, d), lambda i: (i, 0))
    hbm_spec = pl.BlockSpec(memory_space=pl.ANY)
    in_specs = [row_spec, _const_spec((1, d)), hbm_spec, hbm_spec, hbm_spec]
    args = [x, g, wg, wu, wd]
    if final:
        in_specs.append(_const_spec((1, d)))
        args.append(final_g)
    vmem = (3 * d * d_ff * 2 + 2 * (in_chunk * d_ff + out_chunk * d) * 4 + 4 * block * d * 4 + block * d * 2
            + block * d_ff * 2 + 4 * rows * cols * 4)
    return pl.pallas_call(
        functools.partial(_ffn_kernel, cols=cols, final=final),
        grid=(s // block,),
        in_specs=in_specs,
        out_specs=row_spec,
        out_shape=jax.ShapeDtypeStruct((s, d), _F32),
        scratch_shapes=[
            pltpu.VMEM((n_sub, rows, d), _BF16), pltpu.VMEM((n_sub, rows, d_ff), _BF16),
            pltpu.VMEM((d, d_ff), _BF16), pltpu.VMEM((d, d_ff), _BF16), pltpu.VMEM((d_ff, d), _BF16),
            pltpu.VMEM((2, in_chunk, d_ff), _F32),
            pltpu.VMEM((2, out_chunk, d), _F32),
            pltpu.SemaphoreType.DMA((4,))],
        compiler_params=pltpu.CompilerParams(dimension_semantics=("arbitrary",),
                                             vmem_limit_bytes=_vmem_limit(vmem)),
        name="swiglu_final" if final else "swiglu",
    )(*args)


def _gate_terms(gates, m_in, chunk):
    rows = gates.shape[0]
    g8 = gates.T[0:2 * N_HEADS, :]
    ig = g8[0:N_HEADS, :]
    t_in_chunk = lax.broadcasted_iota(jnp.int32, ig.shape, 1) % chunk
    b = jax.nn.log_sigmoid(g8[N_HEADS:2 * N_HEADS, :])
    shift = 1
    while shift < chunk:
        b = b + jnp.where(t_in_chunk >= shift, pltpu.roll(b, shift, axis=1), 0.0)
        shift *= 2
    drow = ig - b
    cm = drow
    shift = 1
    while shift < chunk:
        cm = jnp.maximum(cm, jnp.where(t_in_chunk >= shift, pltpu.roll(cm, shift, axis=1), -jnp.inf))
        shift *= 2
    neg_mx, inter_w, exp_neg_m, w_state, decays = [], [], [], [], []
    m_prev = m_in
    for c in range(rows // chunk):
        sl = slice(c * chunk, (c + 1) * chunk)
        g = b[:, (c + 1) * chunk - 1:(c + 1) * chunk]
        a = g + drow[:, sl]
        m_new = jnp.maximum(g + m_prev, jnp.max(a, axis=1, keepdims=True))
        mx = jnp.maximum(m_prev, cm[:, sl])
        neg_mx.append(-mx)
        inter_w.append(jnp.exp(m_prev - mx))
        exp_neg_m.append(jnp.exp(-(b[:, sl] + mx)))
        w_state.append(jnp.exp(a - m_new))
        decays.append(jnp.exp(g + m_prev - m_new))
        m_prev = m_new
    cat = lambda parts: jnp.concatenate(parts, axis=1)
    stacked = jnp.concatenate(
        [cat(neg_mx), cat(inter_w), cat(exp_neg_m), cat(w_state), jnp.zeros((LANES - 4 * N_HEADS, rows), _F32)],
        axis=0)
    return stacked.T, drow, decays, m_prev


def _mlstm_chunk(q, k, v, neg_mx, inter_w, exp_neg_m, w_state, d_row, decay, c_prev, n_prev):
    length = q.shape[0]
    qb, kb, vb = q.astype(_BF16), k.astype(_BF16), v.astype(_BF16)
    row_id = lax.broadcasted_iota(jnp.int32, (length, length), 0)
    col_id = lax.broadcasted_iota(jnp.int32, (length, length), 1)
    dw = jnp.exp(jnp.where(row_id >= col_id, neg_mx + d_row, -jnp.inf))
    scores = lax.dot_general(qb, kb, (((1,), (1,)), ((), ())), preferred_element_type=_F32) * dw
    num = (jnp.dot(scores.astype(_BF16), vb, preferred_element_type=_F32)
           + inter_w * jnp.dot(qb, c_prev.astype(_BF16), preferred_element_type=_F32))
    qn = (jnp.sum(scores, axis=1, keepdims=True)
          + inter_w * jnp.sum(q * n_prev, axis=1, keepdims=True))
    h = num / jnp.maximum(jnp.abs(qn), exp_neg_m)
    kw = k * w_state
    c_new = decay * c_prev + lax.dot_general(kw.astype(_BF16), vb, (((0,), (0,)), ((), ())),
                                             preferred_element_type=_F32)
    n_new = decay * n_prev + jnp.sum(kw, axis=0, keepdims=True)
    return h, c_new, n_new


def _project(x, g_ref, win_ref, proj_ref):
    hb = _rmsnorm(x, g_ref[...]).astype(_BF16)
    proj_ref[HALO:, :] = jnp.dot(hb, win_ref[...], preferred_element_type=_F32)


def _mix_tile(proj_ref, x_ref, o_ref, row0, first_pos, *, bgate_ref, convw_ref, convb_ref, mhn_ref, poolw_ref,
              pscale_ref, wout_ref, qk_scr, mix_scr, c_scr, n_scr, m_scr, rows, chunk):
    width = N_HEADS * HEAD_DIM
    off_pool = 4 * width
    off_gate = off_pool + len(POOL_WINDOWS) * POOL_GROUP_DIM

    for rb in range(rows // CONV_BLOCK):
        base = HALO + rb * CONV_BLOCK
        conv = convb_ref[...] + convw_ref[CONV_WIDTH - 1:CONV_WIDTH, :] * proj_ref[base:base + CONV_BLOCK, 0:2 * width]
        for j in range(CONV_WIDTH - 1):
            off = base - (CONV_WIDTH - 1) + j
            conv = conv + convw_ref[j:j + 1, :] * proj_ref[off:off + CONV_BLOCK, 0:2 * width]
        qk_scr[rb * CONV_BLOCK:(rb + 1) * CONV_BLOCK, :] = conv * jax.nn.sigmoid(conv)

    gates = proj_ref[HALO:, off_gate:off_gate + LANES] + bgate_ref[...]
    cols, drow, decays, m_out = _gate_terms(gates, m_scr[0:N_HEADS, 0:1], chunk)
    m_scr[0:N_HEADS, 0:1] = m_out

    for c in range(rows // chunk):
        rs = slice(c * chunk, (c + 1) * chunk)
        rh = slice(HALO + c * chunk, HALO + (c + 1) * chunk)
        for hd in range(N_HEADS):
            hs = slice(hd * HEAD_DIM, (hd + 1) * HEAD_DIM)
            q = qk_scr[rs, hs]
            k = qk_scr[rs, width + hd * HEAD_DIM: width + (hd + 1) * HEAD_DIM] * (HEAD_DIM ** -0.5)
            v = proj_ref[rh, 2 * width + hd * HEAD_DIM: 2 * width + (hd + 1) * HEAD_DIM]
            og = proj_ref[rh, 3 * width + hd * HEAD_DIM: 3 * width + (hd + 1) * HEAD_DIM]
            h, c_new, n_new = _mlstm_chunk(
                q, k, v,
                cols[rs, hd:hd + 1], cols[rs, N_HEADS + hd:N_HEADS + hd + 1],
                cols[rs, 2 * N_HEADS + hd:2 * N_HEADS + hd + 1], cols[rs, 3 * N_HEADS + hd:3 * N_HEADS + hd + 1],
                drow[hd:hd + 1, rs], decays[c][hd:hd + 1, :], c_scr[hd], n_scr[hd])
            c_scr[hd] = c_new
            n_scr[hd] = n_new
            h = h * lax.rsqrt(jnp.mean(h * h, axis=-1, keepdims=True) + EPS)
            mix_scr[rs, hs] = (jax.nn.sigmoid(og) * (h * mhn_ref[:, hs])).astype(_BF16)

    for rb in range(rows // POOL_BLOCK):
        r0 = rb * POOL_BLOCK
        pos = first_pos + r0 + lax.broadcasted_iota(jnp.int32, (POOL_BLOCK, 1), 0) + 1
        for gi, win in enumerate(POOL_WINDOWS):
            cs = slice(off_pool + gi * POOL_GROUP_DIM, off_pool + (gi + 1) * POOL_GROUP_DIM)
            total = proj_ref[r0:r0 + HALO + POOL_BLOCK, cs]
            span = 1
            while span < win:
                total = total + pltpu.roll(total, span, axis=0)
                span *= 2
            inv_count = 1.0 / jnp.minimum(pos, win).astype(_F32)
            pooled = total[HALO:, :] * inv_count - proj_ref[HALO + r0:HALO + r0 + POOL_BLOCK, cs]
            mixed = jnp.dot(pooled.astype(_BF16), poolw_ref[gi], preferred_element_type=_F32)
            mix_scr[r0:r0 + POOL_BLOCK, width + gi * POOL_GROUP_DIM: width + (gi + 1) * POOL_GROUP_DIM] = (
                mixed * pscale_ref[:, gi * POOL_GROUP_DIM:(gi + 1) * POOL_GROUP_DIM]).astype(_BF16)

    o_ref[row0:row0 + rows, :] = x_ref[row0:row0 + rows, :] + jnp.dot(
        mix_scr[...], wout_ref[...], preferred_element_type=_F32)


def _mixer_kernel(x_ref, xn_ref, g_ref, win_ref, bgate_ref, convw_ref, convb_ref, mhn_ref, poolw_ref,
                  pscale_ref, wout_ref, o_ref, proj_a, proj_b, qk_scr, mix_scr, c_scr, n_scr, m_scr, *, chunk):
    rows = xn_ref.shape[0]
    step = pl.program_id(0)
    tile = functools.partial(
        _mix_tile, bgate_ref=bgate_ref, convw_ref=convw_ref, convb_ref=convb_ref, mhn_ref=mhn_ref,
        poolw_ref=poolw_ref, pscale_ref=pscale_ref, wout_ref=wout_ref, qk_scr=qk_scr, mix_scr=mix_scr,
        c_scr=c_scr, n_scr=n_scr, m_scr=m_scr, rows=rows, chunk=chunk)

    @pl.when(step == 0)
    def _():
        proj_a[0:HALO, :] = jnp.zeros((HALO, proj_a.shape[1]), _F32)
        c_scr[...] = jnp.zeros(c_scr.shape, _F32)
        n_scr[...] = jnp.zeros(n_scr.shape, _F32)
        m_scr[...] = jnp.zeros(m_scr.shape, _F32)
        _project(x_ref[0:rows, :], g_ref, win_ref, proj_a)

    _project(x_ref[rows:2 * rows, :], g_ref, win_ref, proj_b)
    proj_b[0:HALO, :] = proj_a[rows:rows + HALO, :]
    tile(proj_a, x_ref, o_ref, 0, step * (2 * rows))
    _project(xn_ref[...], g_ref, win_ref, proj_a)
    proj_a[0:HALO, :] = proj_b[rows:rows + HALO, :]
    tile(proj_b, x_ref, o_ref, rows, step * (2 * rows) + rows)


def _mixer_call(x, g, win, bgate, convw, convb, mhn, poolw, pscale, wout):
    s, d = x.shape
    rows, chunk = MIX_ROWS, MIX_CHUNK
    width = N_HEADS * HEAD_DIM
    cols = win.shape[1]
    assert s % (2 * rows) == 0 and rows % chunk == 0 and chunk % LANES == 0
    assert rows % CONV_BLOCK == 0 and rows % POOL_BLOCK == 0
    n_steps = s // (2 * rows)
    pair_spec = pl.BlockSpec((2 * rows, d), lambda i: (i, 0))
    next_spec = pl.BlockSpec((rows, d), lambda i: (jnp.minimum(2 * i + 2, 2 * n_steps - 1), 0))
    consts = [g, win, bgate, convw, convb, mhn, poolw, pscale, wout]
    in_specs = [pair_spec, next_spec] + [_const_spec(a.shape) for a in consts]
    vmem = (sum(a.size * a.dtype.itemsize for a in consts) + 10 * rows * d * 4
            + 2 * (rows + HALO) * cols * 4 + rows * 2 * width * 6 + 4 * rows * cols * 4)
    return pl.pallas_call(
        functools.partial(_mixer_kernel, chunk=chunk),
        grid=(n_steps,),
        in_specs=in_specs,
        out_specs=pair_spec,
        out_shape=jax.ShapeDtypeStruct((s, d), _F32),
        scratch_shapes=[
            pltpu.VMEM((rows + HALO, cols), _F32),
            pltpu.VMEM((rows + HALO, cols), _F32),
            pltpu.VMEM((rows, 2 * width), _F32),
            pltpu.VMEM((rows, 2 * width), _BF16),
            pltpu.VMEM((N_HEADS, HEAD_DIM, HEAD_DIM), _F32),
            pltpu.VMEM((N_HEADS, 1, HEAD_DIM), _F32),
            pltpu.VMEM((8, LANES), _F32),
        ],
        compiler_params=pltpu.CompilerParams(dimension_semantics=("arbitrary",),
                                             vmem_limit_bytes=_vmem_limit(vmem)),
        name="mixer",
    )(x, x, *consts)


def kernel(x, ffn1_norm, ffn1_w_gate, ffn1_w_up, ffn1_w_down, mix_norm, w_in, b_gates, conv_w, conv_b,
           mh_norm, pool_w, pool_scale, w_out, ffn2_norm, ffn2_w_gate, ffn2_w_up, ffn2_w_down, final_norm):
    bsz, s, d = x.shape
    depth = ffn1_norm.shape[0]
    assert depth >= 1
    width = N_HEADS * HEAD_DIM
    off_g = 4 * width
    off_p = off_g + 2 * N_HEADS
    bf = lambda a: a.astype(_BF16)
    outs = []
    for b in range(bsz):
        xb = x[b]
        for l in range(depth):
            xb = _ffn_call(xb, ffn1_norm[l][None, :], ffn1_w_gate[l], ffn1_w_up[l], ffn1_w_down[l])
            win = jnp.concatenate(
                [w_in[l][:, :off_g], w_in[l][:, off_p:],
                 jnp.pad(w_in[l][:, off_g:off_p], ((0, 0), (0, LANES - 2 * N_HEADS)))], axis=1)
            bgate = jnp.pad(b_gates[l], (0, LANES - 2 * N_HEADS))[None, :]
            xb = _mixer_call(
                xb, mix_norm[l][None, :], bf(win), bgate, conv_w[l], conv_b[l][None, :], mh_norm[l][None, :],
                bf(pool_w[l]), pool_scale[l][None, :], bf(w_out[l]))
            last = l == depth - 1
            xb = _ffn_call(xb, ffn2_norm[l][None, :], ffn2_w_gate[l], ffn2_w_up[l], ffn2_w_down[l],
                           final_g=final_norm[None, :] if last else None)
        outs.append(xb)
    return jnp.stack(outs, axis=0)
```

```python
import functools

import jax
import jax.numpy as jnp
from jax import lax
from jax.experimental import pallas as pl
from jax.experimental.pallas import tpu as pltpu

EPS = 1e-6
N_HEADS = 4
HEAD_DIM = 128
POOL_WINDOWS = (2, 4, 8, 16)
POOL_GROUP_DIM = 128
CONV_WIDTH = 4

LANES = 128
HALO = 16
V7X_VMEM_BYTES = 64 * 1024 * 1024

FFN_ROWS = 512
FFN_SUBTILES = 2
FFN_COLS = 256
MIX_ROWS = 512
MIX_CHUNK = 256
CONV_BLOCK = 32
POOL_BLOCK = 128

_F32 = jnp.float32
_BF16 = jnp.bfloat16


def _rmsnorm(x, g):
    y = x * lax.rsqrt(jnp.mean(x * x, axis=-1, keepdims=True) + EPS)
    return y * g


def _const_spec(shape):
    return pl.BlockSpec(shape, lambda i: (0,) * len(shape), pipeline_mode=pl.Buffered(1))


def _vmem_limit(nbytes):
    return int(min(nbytes * 1.25 + (8 << 20), V7X_VMEM_BYTES - (4 << 20)))


def _weight_copies(wg_hbm, wu_hbm, wd_hbm, stage_in, stage_out, sems, cols):
    n_chunks = wg_hbm.shape[1] // cols
    copies = []
    for j in range(n_chunks):
        cs = pl.ds(j * cols, cols)
        p = j % 2
        copies.append((
            pltpu.make_async_copy(wg_hbm.at[:, cs], stage_in.at[2 * p], sems.at[3 * p]),
            pltpu.make_async_copy(wu_hbm.at[:, cs], stage_in.at[2 * p + 1], sems.at[3 * p + 1]),
            pltpu.make_async_copy(wd_hbm.at[cs, :], stage_out.at[p], sems.at[3 * p + 2])))
    return copies


def _ffn_kernel(x_ref, g_ref, wg_hbm, wu_hbm, wd_hbm, *rest, cols, final):
    if final:
        fg_ref, o_ref, h_scr, act_scr, wg_ref, wu_ref, wd_ref, stage_in, stage_out, sems = rest
    else:
        o_ref, h_scr, act_scr, wg_ref, wu_ref, wd_ref, stage_in, stage_out, sems = rest
    d, d_ff = wg_ref.shape
    n_chunks = d_ff // cols
    n_sub, rows = h_scr.shape[0], h_scr.shape[1]
    step = pl.program_id(0)

    def sub_tile(t, before_chunk=None):
        for j in range(n_chunks):
            if before_chunk is not None:
                before_chunk(j)
            sl = slice(j * cols, (j + 1) * cols)
            h = h_scr[t]
            gate = jnp.dot(h, wg_ref[:, sl], preferred_element_type=_F32)
            up = jnp.dot(h, wu_ref[:, sl], preferred_element_type=_F32)
            act_scr[t, :, sl] = (gate * jax.nn.sigmoid(gate) * up).astype(_BF16)
        down = jnp.dot(act_scr[t], wd_ref[...], preferred_element_type=_F32)
        y = x_ref[t * rows:(t + 1) * rows, :] + 0.5 * down
        if final:
            y = _rmsnorm(y, fg_ref[...])
        o_ref[t * rows:(t + 1) * rows, :] = y

    def norms():
        for t in range(n_sub):
            h_scr[t] = _rmsnorm(x_ref[t * rows:(t + 1) * rows, :], g_ref[...]).astype(_BF16)

    @pl.when(step == 0)
    def _():
        copies = _weight_copies(wg_hbm, wu_hbm, wd_hbm, stage_in, stage_out, sems, cols)

        def start(j):
            for cp in copies[j]:
                cp.start()

        def land(j):
            sl = slice(j * cols, (j + 1) * cols)
            p = j % 2
            for cp in copies[j]:
                cp.wait()
            wg_ref[:, sl] = stage_in[2 * p].astype(_BF16)
            wu_ref[:, sl] = stage_in[2 * p + 1].astype(_BF16)
            wd_ref[sl, :] = stage_out[p].astype(_BF16)
            if j + 2 < n_chunks:
                start(j + 2)

        start(0)
        if n_chunks > 1:
            start(1)
        norms()
        sub_tile(0, before_chunk=land)
        for t in range(1, n_sub):
            sub_tile(t)

    @pl.when(step != 0)
    def _():
        norms()
        for t in range(n_sub):
            sub_tile(t)


def _ffn_call(x, g, wg, wu, wd, final_g=None):
    s, d = x.shape
    d_ff = wg.shape[1]
    rows, cols, n_sub = FFN_ROWS, FFN_COLS, FFN_SUBTILES
    block = n_sub * rows
    assert s % block == 0 and d_ff % cols == 0 and cols % LANES == 0
    final = final_g is not None
    row_spec = pl.BlockSpec((block, d), lambda i: (i, 0))
    hbm_spec = pl.BlockSpec(memory_space=pl.ANY)
    in_specs = [row_spec, _const_spec((1, d)), hbm_spec, hbm_spec, hbm_spec]
    args = [x, g, wg, wu, wd]
    if final:
        in_specs.append(_const_spec((1, d)))
        args.append(final_g)
    vmem = (3 * d * d_ff * 2 + 6 * d * cols * 4 + 4 * block * d * 4 + block * d * 2
            + block * d_ff * 2 + 4 * rows * cols * 4)
    return pl.pallas_call(
        functools.partial(_ffn_kernel, cols=cols, final=final),
        grid=(s // block,),
        in_specs=in_specs,
        out_specs=row_spec,
        out_shape=jax.ShapeDtypeStruct((s, d), _F32),
        scratch_shapes=[
            pltpu.VMEM((n_sub, rows, d), _BF16), pltpu.VMEM((n_sub, rows, d_ff), _BF16),
            pltpu.VMEM((d, d_ff), _BF16), pltpu.VMEM((d, d_ff), _BF16), pltpu.VMEM((d_ff, d), _BF16),
            pltpu.VMEM((4, d, cols), _F32),
            pltpu.VMEM((2, cols, d), _F32),
            pltpu.SemaphoreType.DMA((6,))],
        compiler_params=pltpu.CompilerParams(dimension_semantics=("arbitrary",),
                                             vmem_limit_bytes=_vmem_limit(vmem)),
        name="swiglu_final" if final else "swiglu",
    )(*args)


def _gate_terms(gates, m_in, chunk):
    rows = gates.shape[0]
    g8 = gates.T[0:2 * N_HEADS, :]
    ig = g8[0:N_HEADS, :]
    t_in_chunk = lax.broadcasted_iota(jnp.int32, ig.shape, 1) % chunk
    b = jax.nn.log_sigmoid(g8[N_HEADS:2 * N_HEADS, :])
    shift = 1
    while shift < chunk:
        b = b + jnp.where(t_in_chunk >= shift, pltpu.roll(b, shift, axis=1), 0.0)
        shift *= 2
    drow = ig - b
    cm = drow
    shift = 1
    while shift < chunk:
        cm = jnp.maximum(cm, jnp.where(t_in_chunk >= shift, pltpu.roll(cm, shift, axis=1), -jnp.inf))
        shift *= 2
    neg_mx, inter_w, exp_neg_m, w_state, decays = [], [], [], [], []
    m_prev = m_in
    for c in range(rows // chunk):
        sl = slice(c * chunk, (c + 1) * chunk)
        g = b[:, (c + 1) * chunk - 1:(c + 1) * chunk]
        a = g + drow[:, sl]
        m_new = jnp.maximum(g + m_prev, jnp.max(a, axis=1, keepdims=True))
        mx = jnp.maximum(m_prev, cm[:, sl])
        neg_mx.append(-mx)
        inter_w.append(jnp.exp(m_prev - mx))
        exp_neg_m.append(jnp.exp(-(b[:, sl] + mx)))
        w_state.append(jnp.exp(a - m_new))
        decays.append(jnp.exp(g + m_prev - m_new))
        m_prev = m_new
    cat = lambda parts: jnp.concatenate(parts, axis=1)
    stacked = jnp.concatenate(
        [cat(neg_mx), cat(inter_w), cat(exp_neg_m), cat(w_state), jnp.zeros((LANES - 4 * N_HEADS, rows), _F32)],
        axis=0)
    return stacked.T, drow, decays, m_prev


def _mlstm_chunk(q, k, v, neg_mx, inter_w, exp_neg_m, w_state, d_row, decay, c_prev, n_prev):
    length = q.shape[0]
    qb, kb, vb = q.astype(_BF16), k.astype(_BF16), v.astype(_BF16)
    row_id = lax.broadcasted_iota(jnp.int32, (length, length), 0)
    col_id = lax.broadcasted_iota(jnp.int32, (length, length), 1)
    dw = jnp.exp(jnp.where(row_id >= col_id, neg_mx + d_row, -jnp.inf))
    scores = lax.dot_general(qb, kb, (((1,), (1,)), ((), ())), preferred_element_type=_F32) * dw
    num = (jnp.dot(scores.astype(_BF16), vb, preferred_element_type=_F32)
           + inter_w * jnp.dot(qb, c_prev.astype(_BF16), preferred_element_type=_F32))
    qn = (jnp.sum(scores, axis=1, keepdims=True)
          + inter_w * jnp.sum(q * n_prev, axis=1, keepdims=True))
    h = num / jnp.maximum(jnp.abs(qn), exp_neg_m)
    kw = k * w_state
    c_new = decay * c_prev + lax.dot_general(kw.astype(_BF16), vb, (((0,), (0,)), ((), ())),
                                             preferred_element_type=_F32)
    n_new = decay * n_prev + jnp.sum(kw, axis=0, keepdims=True)
    return h, c_new, n_new


def _project(x, g_ref, win_ref, proj_ref):
    hb = _rmsnorm(x, g_ref[...]).astype(_BF16)
    proj_ref[HALO:, :] = jnp.dot(hb, win_ref[...], preferred_element_type=_F32)


def _mix_tile(proj_ref, x_ref, o_ref, row0, first_pos, *, bgate_ref, convw_ref, convb_ref, mhn_ref, poolw_ref,
              pscale_ref, wout_ref, qk_scr, mix_scr, c_scr, n_scr, m_scr, rows, chunk):
    width = N_HEADS * HEAD_DIM
    off_pool = 4 * width
    off_gate = off_pool + len(POOL_WINDOWS) * POOL_GROUP_DIM

    for rb in range(rows // CONV_BLOCK):
        base = HALO + rb * CONV_BLOCK
        conv = convb_ref[...] + convw_ref[CONV_WIDTH - 1:CONV_WIDTH, :] * proj_ref[base:base + CONV_BLOCK, 0:2 * width]
        for j in range(CONV_WIDTH - 1):
            off = base - (CONV_WIDTH - 1) + j
            conv = conv + convw_ref[j:j + 1, :] * proj_ref[off:off + CONV_BLOCK, 0:2 * width]
        qk_scr[rb * CONV_BLOCK:(rb + 1) * CONV_BLOCK, :] = conv * jax.nn.sigmoid(conv)

    gates = proj_ref[HALO:, off_gate:off_gate + LANES] + bgate_ref[...]
    cols, drow, decays, m_out = _gate_terms(gates, m_scr[0:N_HEADS, 0:1], chunk)
    m_scr[0:N_HEADS, 0:1] = m_out

    for c in range(rows // chunk):
        rs = slice(c * chunk, (c + 1) * chunk)
        rh = slice(HALO + c * chunk, HALO + (c + 1) * chunk)
        for hd in range(N_HEADS):
            hs = slice(hd * HEAD_DIM, (hd + 1) * HEAD_DIM)
            q = qk_scr[rs, hs]
            k = qk_scr[rs, width + hd * HEAD_DIM: width + (hd + 1) * HEAD_DIM] * (HEAD_DIM ** -0.5)
            v = proj_ref[rh, 2 * width + hd * HEAD_DIM: 2 * width + (hd + 1) * HEAD_DIM]
            og = proj_ref[rh, 3 * width + hd * HEAD_DIM: 3 * width + (hd + 1) * HEAD_DIM]
            h, c_new, n_new = _mlstm_chunk(
                q, k, v,
                cols[rs, hd:hd + 1], cols[rs, N_HEADS + hd:N_HEADS + hd + 1],
                cols[rs, 2 * N_HEADS + hd:2 * N_HEADS + hd + 1], cols[rs, 3 * N_HEADS + hd:3 * N_HEADS + hd + 1],
                drow[hd:hd + 1, rs], decays[c][hd:hd + 1, :], c_scr[hd], n_scr[hd])
            c_scr[hd] = c_new
            n_scr[hd] = n_new
            h = h * lax.rsqrt(jnp.mean(h * h, axis=-1, keepdims=True) + EPS)
            mix_scr[rs, hs] = (jax.nn.sigmoid(og) * (h * mhn_ref[:, hs])).astype(_BF16)

    for rb in range(rows // POOL_BLOCK):
        r0 = rb * POOL_BLOCK
        pos = first_pos + r0 + lax.broadcasted_iota(jnp.int32, (POOL_BLOCK, 1), 0) + 1
        for gi, win in enumerate(POOL_WINDOWS):
            cs = slice(off_pool + gi * POOL_GROUP_DIM, off_pool + (gi + 1) * POOL_GROUP_DIM)
            total = proj_ref[r0:r0 + HALO + POOL_BLOCK, cs]
            span = 1
            while span < win:
                total = total + pltpu.roll(total, span, axis=0)
                span *= 2
            inv_count = 1.0 / jnp.minimum(pos, win).astype(_F32)
            pooled = total[HALO:, :] * inv_count - proj_ref[HALO + r0:HALO + r0 + POOL_BLOCK, cs]
            mixed = jnp.dot(pooled.astype(_BF16), poolw_ref[gi], preferred_element_type=_F32)
            mix_scr[r0:r0 + POOL_BLOCK, width + gi * POOL_GROUP_DIM: width + (gi + 1) * POOL_GROUP_DIM] = (
                mixed * pscale_ref[:, gi * POOL_GROUP_DIM:(gi + 1) * POOL_GROUP_DIM]).astype(_BF16)

    o_ref[row0:row0 + rows, :] = x_ref[row0:row0 + rows, :] + jnp.dot(
        mix_scr[...], wout_ref[...], preferred_element_type=_F32)


def _mixer_kernel(x_ref, xn_ref, g_ref, win_ref, bgate_ref, convw_ref, convb_ref, mhn_ref, poolw_ref,
                  pscale_ref, wout_ref, o_ref, proj_a, proj_b, qk_scr, mix_scr, c_scr, n_scr, m_scr, *, chunk):
    rows = xn_ref.shape[0]
    step = pl.program_id(0)
    tile = functools.partial(
        _mix_tile, bgate_ref=bgate_ref, convw_ref=convw_ref, convb_ref=convb_ref, mhn_ref=mhn_ref,
        poolw_ref=poolw_ref, pscale_ref=pscale_ref, wout_ref=wout_ref, qk_scr=qk_scr, mix_scr=mix_scr,
        c_scr=c_scr, n_scr=n_scr, m_scr=m_scr, rows=rows, chunk=chunk)

    @pl.when(step == 0)
    def _():
        proj_a[0:HALO, :] = jnp.zeros((HALO, proj_a.shape[1]), _F32)
        c_scr[...] = jnp.zeros(c_scr.shape, _F32)
        n_scr[...] = jnp.zeros(n_scr.shape, _F32)
        m_scr[...] = jnp.zeros(m_scr.shape, _F32)
        _project(x_ref[0:rows, :], g_ref, win_ref, proj_a)

    _project(x_ref[rows:2 * rows, :], g_ref, win_ref, proj_b)
    proj_b[0:HALO, :] = proj_a[rows:rows + HALO, :]
    tile(proj_a, x_ref, o_ref, 0, step * (2 * rows))
    _project(xn_ref[...], g_ref, win_ref, proj_a)
    proj_a[0:HALO, :] = proj_b[rows:rows + HALO, :]
    tile(proj_b, x_ref, o_ref, rows, step * (2 * rows) + rows)


def _mixer_call(x, g, win, bgate, convw, convb, mhn, poolw, pscale, wout):
    s, d = x.shape
    rows, chunk = MIX_ROWS, MIX_CHUNK
    width = N_HEADS * HEAD_DIM
    cols = win.shape[1]
    assert s % (2 * rows) == 0 and rows % chunk == 0 and chunk % LANES == 0
    assert rows % CONV_BLOCK == 0 and rows % POOL_BLOCK == 0
    n_steps = s // (2 * rows)
    pair_spec = pl.BlockSpec((2 * rows, d), lambda i: (i, 0))
    next_spec = pl.BlockSpec((rows, d), lambda i: (jnp.minimum(2 * i + 2, 2 * n_steps - 1), 0))
    consts = [g, win, bgate, convw, convb, mhn, poolw, pscale, wout]
    in_specs = [pair_spec, next_spec] + [_const_spec(a.shape) for a in consts]
    vmem = (sum(a.size * a.dtype.itemsize for a in consts) + 10 * rows * d * 4
            + 2 * (rows + HALO) * cols * 4 + rows * 2 * width * 6 + 4 * rows * cols * 4)
    return pl.pallas_call(
        functools.partial(_mixer_kernel, chunk=chunk),
        grid=(n_steps,),
        in_specs=in_specs,
        out_specs=pair_spec,
        out_shape=jax.ShapeDtypeStruct((s, d), _F32),
        scratch_shapes=[
            pltpu.VMEM((rows + HALO, cols), _F32),
            pltpu.VMEM((rows + HALO, cols), _F32),
            pltpu.VMEM((rows, 2 * width), _F32),
            pltpu.VMEM((rows, 2 * width), _BF16),
            pltpu.VMEM((N_HEADS, HEAD_DIM, HEAD_DIM), _F32),
            pltpu.VMEM((N_HEADS, 1, HEAD_DIM), _F32),
            pltpu.VMEM((8, LANES), _F32),
        ],
        compiler_params=pltpu.CompilerParams(dimension_semantics=("arbitrary",),
                                             vmem_limit_bytes=_vmem_limit(vmem)),
        name="mixer",
    )(x, x, *consts)


def kernel(x, ffn1_norm, ffn1_w_gate, ffn1_w_up, ffn1_w_down, mix_norm, w_in, b_gates, conv_w, conv_b,
           mh_norm, pool_w, pool_scale, w_out, ffn2_norm, ffn2_w_gate, ffn2_w_up, ffn2_w_down, final_norm):
    bsz, s, d = x.shape
    depth = ffn1_norm.shape[0]
    assert depth >= 1
    width = N_HEADS * HEAD_DIM
    off_g = 4 * width
    off_p = off_g + 2 * N_HEADS
    bf = lambda a: a.astype(_BF16)
    outs = []
    for b in range(bsz):
        xb = x[b]
        for l in range(depth):
            xb = _ffn_call(xb, ffn1_norm[l][None, :], ffn1_w_gate[l], ffn1_w_up[l], ffn1_w_down[l])
            win = jnp.concatenate(
                [w_in[l][:, :off_g], w_in[l][:, off_p:],
                 jnp.pad(w_in[l][:, off_g:off_p], ((0, 0), (0, LANES - 2 * N_HEADS)))], axis=1)
            bgate = jnp.pad(b_gates[l], (0, LANES - 2 * N_HEADS))[None, :]
            xb = _mixer_call(
                xb, mix_norm[l][None, :], bf(win), bgate, conv_w[l], conv_b[l][None, :], mh_norm[l][None, :],
                bf(pool_w[l]), pool_scale[l][None, :], bf(w_out[l]))
            last = l == depth - 1
            xb = _ffn_call(xb, ffn2_norm[l][None, :], ffn2_w_gate[l], ffn2_w_up[l], ffn2_w_down[l],
                           final_g=final_norm[None, :] if last else None)
        outs.append(xb)
    return jnp.stack(outs, axis=0)
```

```python
import functools

import jax
import jax.numpy as jnp
from jax import lax
from jax.experimental import pallas as pl
from jax.experimental.pallas import tpu as pltpu

EPS = 1e-6
N_HEADS = 4
HEAD_DIM = 128
POOL_WINDOWS = (2, 4, 8, 16)
POOL_GROUP_DIM = 128
CONV_WIDTH = 4

LANES = 128
HALO = 16
V7X_VMEM_BYTES = 64 * 1024 * 1024

FFN_ROWS = 256
FFN_SUBTILES = 1
FFN_COLS = 256
WEIGHT_STAGE_SPLIT = 4
MIX_ROWS = 512
MIX_CHUNK = 256
CONV_BLOCK = 32
POOL_BLOCK = 128

_F32 = jnp.float32
_BF16 = jnp.bfloat16


def _rmsnorm(x, g):
    y = x * lax.rsqrt(jnp.mean(x * x, axis=-1, keepdims=True) + EPS)
    return y * g


def _const_spec(shape):
    return pl.BlockSpec(shape, lambda i: (0,) * len(shape), pipeline_mode=pl.Buffered(1))


def _vmem_limit(nbytes):
    return int(min(nbytes * 1.25 + (8 << 20), V7X_VMEM_BYTES - (4 << 20)))


def _stage_weights(jobs):
    copies = [pltpu.make_async_copy(src, stage, sem) for src, stage, sem, _ in jobs]
    copies[0].start()
    for i, (_, stage, _, dst) in enumerate(jobs):
        if i + 1 < len(jobs):
            copies[i + 1].start()
        copies[i].wait()
        dst[...] = stage[...].astype(_BF16)


def _ffn_kernel(x_ref, g_ref, wg_hbm, wu_hbm, wd_hbm, *rest, cols, final):
    if final:
        fg_ref, o_ref, h_scr, act_scr, wg_ref, wu_ref, wd_ref, stage_in, stage_out, sems = rest
    else:
        o_ref, h_scr, act_scr, wg_ref, wu_ref, wd_ref, stage_in, stage_out, sems = rest
    d, d_ff = wg_ref.shape

    @pl.when(pl.program_id(0) == 0)
    def _():
        jobs = []
        for w_hbm, w_ref, stage, sem0 in ((wg_hbm, wg_ref, stage_in, 0), (wu_hbm, wu_ref, stage_in, 0),
                                          (wd_hbm, wd_ref, stage_out, 2)):
            chunk = stage.shape[1]
            for c in range(w_hbm.shape[0] // chunk):
                rows = pl.ds(c * chunk, chunk)
                slot = len(jobs) % 2
                jobs.append((w_hbm.at[rows], stage.at[slot], sems.at[sem0 + slot], w_ref.at[rows]))
        _stage_weights(jobs)

    n_sub, rows = h_scr.shape[0], h_scr.shape[1]
    for t in range(n_sub):
        h_scr[t] = _rmsnorm(x_ref[t * rows:(t + 1) * rows, :], g_ref[...]).astype(_BF16)
    for t in range(n_sub):
        for j in range(d_ff // cols):
            sl = slice(j * cols, (j + 1) * cols)
            h = h_scr[t]
            gate = jnp.dot(h, wg_ref[:, sl], preferred_element_type=_F32)
            up = jnp.dot(h, wu_ref[:, sl], preferred_element_type=_F32)
            act_scr[t, :, sl] = (gate * jax.nn.sigmoid(gate) * up).astype(_BF16)
        down = jnp.dot(act_scr[t], wd_ref[...], preferred_element_type=_F32)
        y = x_ref[t * rows:(t + 1) * rows, :] + 0.5 * down
        if final:
            y = _rmsnorm(y, fg_ref[...])
        o_ref[t * rows:(t + 1) * rows, :] = y


def _ffn_call(x, g, wg, wu, wd, final_g=None):
    s, d = x.shape
    d_ff = wg.shape[1]
    rows, cols, n_sub = FFN_ROWS, FFN_COLS, FFN_SUBTILES
    block = n_sub * rows
    in_chunk, out_chunk = d // WEIGHT_STAGE_SPLIT, d_ff // WEIGHT_STAGE_SPLIT
    assert s % block == 0 and d_ff % cols == 0
    assert d % in_chunk == 0 and d_ff % out_chunk == 0 and in_chunk % 16 == 0 and out_chunk % 16 == 0
    assert WEIGHT_STAGE_SPLIT % 2 == 0
    final = final_g is not None
    row_spec = pl.BlockSpec((block, d), lambda i: (i, 0))
    hbm_spec = pl.BlockSpec(memory_space=pl.ANY)
    in_specs = [row_spec, _const_spec((1, d)), hbm_spec, hbm_spec, hbm_spec]
    args = [x, g, wg, wu, wd]
    if final:
        in_specs.append(_const_spec((1, d)))
        args.append(final_g)
    vmem = (3 * d * d_ff * 2 + 2 * (in_chunk * d_ff + out_chunk * d) * 4 + 4 * block * d * 4 + block * d * 2
            + block * d_ff * 2 + 4 * rows * cols * 4)
    return pl.pallas_call(
        functools.partial(_ffn_kernel, cols=cols, final=final),
        grid=(s // block,),
        in_specs=in_specs,
        out_specs=row_spec,
        out_shape=jax.ShapeDtypeStruct((s, d), _F32),
        scratch_shapes=[
            pltpu.VMEM((n_sub, rows, d), _BF16), pltpu.VMEM((n_sub, rows, d_ff), _BF16),
            pltpu.VMEM((d, d_ff), _BF16), pltpu.VMEM((d, d_ff), _BF16), pltpu.VMEM((d_ff, d), _BF16),
            pltpu.VMEM((2, in_chunk, d_ff), _F32),
            pltpu.VMEM((2, out_chunk, d), _F32),
            pltpu.SemaphoreType.DMA((4,))],
        compiler_params=pltpu.CompilerParams(dimension_semantics=("arbitrary",),
                                             vmem_limit_bytes=_vmem_limit(vmem)),
        name="swiglu_final" if final else "swiglu",
    )(*args)


def _gate_terms(gates, m_in, chunk):
    rows = gates.shape[0]
    g8 = gates.T[0:2 * N_HEADS, :]
    ig = g8[0:N_HEADS, :]
    t_in_chunk = lax.broadcasted_iota(jnp.int32, ig.shape, 1) % chunk
    b = jax.nn.log_sigmoid(g8[N_HEADS:2 * N_HEADS, :])
    shift = 1
    while shift < chunk:
        b = b + jnp.where(t_in_chunk >= shift, pltpu.roll(b, shift, axis=1), 0.0)
        shift *= 2
    drow = ig - b
    cm = drow
    shift = 1
    while shift < chunk:
        cm = jnp.maximum(cm, jnp.where(t_in_chunk >= shift, pltpu.roll(cm, shift, axis=1), -jnp.inf))
        shift *= 2
    neg_mx, inter_w, exp_neg_m, w_state, decays = [], [], [], [], []
    m_prev = m_in
    for c in range(rows // chunk):
        sl = slice(c * chunk, (c + 1) * chunk)
        g = b[:, (c + 1) * chunk - 1:(c + 1) * chunk]
        a = g + drow[:, sl]
        m_new = jnp.maximum(g + m_prev, jnp.max(a, axis=1, keepdims=True))
        mx = jnp.maximum(m_prev, cm[:, sl])
        neg_mx.append(-mx)
        inter_w.append(jnp.exp(m_prev - mx))
        exp_neg_m.append(jnp.exp(-(b[:, sl] + mx)))
        w_state.append(jnp.exp(a - m_new))
        decays.append(jnp.exp(g + m_prev - m_new))
        m_prev = m_new
    cat = lambda parts: jnp.concatenate(parts, axis=1)
    stacked = jnp.concatenate(
        [cat(neg_mx), cat(inter_w), cat(exp_neg_m), cat(w_state), jnp.zeros((LANES - 4 * N_HEADS, rows), _F32)],
        axis=0)
    return stacked.T, drow, decays, m_prev


def _mlstm_chunk(q, k, v, neg_mx, inter_w, exp_neg_m, w_state, d_row, decay, c_prev, n_prev):
    length = q.shape[0]
    qb, kb, vb = q.astype(_BF16), k.astype(_BF16), v.astype(_BF16)
    row_id = lax.broadcasted_iota(jnp.int32, (length, length), 0)
    col_id = lax.broadcasted_iota(jnp.int32, (length, length), 1)
    dw = jnp.exp(jnp.where(row_id >= col_id, neg_mx + d_row, -jnp.inf))
    scores = lax.dot_general(qb, kb, (((1,), (1,)), ((), ())), preferred_element_type=_F32) * dw
    num = (jnp.dot(scores.astype(_BF16), vb, preferred_element_type=_F32)
           + inter_w * jnp.dot(qb, c_prev.astype(_BF16), preferred_element_type=_F32))
    qn = (jnp.sum(scores, axis=1, keepdims=True)
          + inter_w * jnp.sum(q * n_prev, axis=1, keepdims=True))
    h = num / jnp.maximum(jnp.abs(qn), exp_neg_m)
    kw = k * w_state
    c_new = decay * c_prev + lax.dot_general(kw.astype(_BF16), vb, (((0,), (0,)), ((), ())),
                                             preferred_element_type=_F32)
    n_new = decay * n_prev + jnp.sum(kw, axis=0, keepdims=True)
    return h, c_new, n_new


def _project(x, g_ref, win_ref, proj_ref):
    hb = _rmsnorm(x, g_ref[...]).astype(_BF16)
    proj_ref[HALO:, :] = jnp.dot(hb, win_ref[...], preferred_element_type=_F32)


def _mix_tile(proj_ref, x_ref, o_ref, row0, first_pos, *, bgate_ref, convw_ref, convb_ref, mhn_ref, poolw_ref,
              pscale_ref, wout_ref, qk_scr, mix_scr, c_scr, n_scr, m_scr, rows, chunk):
    width = N_HEADS * HEAD_DIM
    off_pool = 4 * width
    off_gate = off_pool + len(POOL_WINDOWS) * POOL_GROUP_DIM

    for rb in range(rows // CONV_BLOCK):
        base = HALO + rb * CONV_BLOCK
        conv = convb_ref[...] + convw_ref[CONV_WIDTH - 1:CONV_WIDTH, :] * proj_ref[base:base + CONV_BLOCK, 0:2 * width]
        for j in range(CONV_WIDTH - 1):
            off = base - (CONV_WIDTH - 1) + j
            conv = conv + convw_ref[j:j + 1, :] * proj_ref[off:off + CONV_BLOCK, 0:2 * width]
        qk_scr[rb * CONV_BLOCK:(rb + 1) * CONV_BLOCK, :] = conv * jax.nn.sigmoid(conv)

    gates = proj_ref[HALO:, off_gate:off_gate + LANES] + bgate_ref[...]
    cols, drow, decays, m_out = _gate_terms(gates, m_scr[0:N_HEADS, 0:1], chunk)
    m_scr[0:N_HEADS, 0:1] = m_out

    for c in range(rows // chunk):
        rs = slice(c * chunk, (c + 1) * chunk)
        rh = slice(HALO + c * chunk, HALO + (c + 1) * chunk)
        for hd in range(N_HEADS):
            hs = slice(hd * HEAD_DIM, (hd + 1) * HEAD_DIM)
            q = qk_scr[rs, hs]
            k = qk_scr[rs, width + hd * HEAD_DIM: width + (hd + 1) * HEAD_DIM] * (HEAD_DIM ** -0.5)
            v = proj_ref[rh, 2 * width + hd * HEAD_DIM: 2 * width + (hd + 1) * HEAD_DIM]
            og = proj_ref[rh, 3 * width + hd * HEAD_DIM: 3 * width + (hd + 1) * HEAD_DIM]
            h, c_new, n_new = _mlstm_chunk(
                q, k, v,
                cols[rs, hd:hd + 1], cols[rs, N_HEADS + hd:N_HEADS + hd + 1],
                cols[rs, 2 * N_HEADS + hd:2 * N_HEADS + hd + 1], cols[rs, 3 * N_HEADS + hd:3 * N_HEADS + hd + 1],
                drow[hd:hd + 1, rs], decays[c][hd:hd + 1, :], c_scr[hd], n_scr[hd])
            c_scr[hd] = c_new
            n_scr[hd] = n_new
            h = h * lax.rsqrt(jnp.mean(h * h, axis=-1, keepdims=True) + EPS)
            mix_scr[rs, hs] = (jax.nn.sigmoid(og) * (h * mhn_ref[:, hs])).astype(_BF16)

    for rb in range(rows // POOL_BLOCK):
        r0 = rb * POOL_BLOCK
        pos = first_pos + r0 + lax.broadcasted_iota(jnp.int32, (POOL_BLOCK, 1), 0) + 1
        for gi, win in enumerate(POOL_WINDOWS):
            cs = slice(off_pool + gi * POOL_GROUP_DIM, off_pool + (gi + 1) * POOL_GROUP_DIM)
            total = proj_ref[r0:r0 + HALO + POOL_BLOCK, cs]
            span = 1
            while span < win:
                total = total + pltpu.roll(total, span, axis=0)
                span *= 2
            inv_count = 1.0 / jnp.minimum(pos, win).astype(_F32)
            pooled = total[HALO:, :] * inv_count - proj_ref[HALO + r0:HALO + r0 + POOL_BLOCK, cs]
            mixed = jnp.dot(pooled.astype(_BF16), poolw_ref[gi], preferred_element_type=_F32)
            mix_scr[r0:r0 + POOL_BLOCK, width + gi * POOL_GROUP_DIM: width + (gi + 1) * POOL_GROUP_DIM] = (
                mixed * pscale_ref[:, gi * POOL_GROUP_DIM:(gi + 1) * POOL_GROUP_DIM]).astype(_BF16)

    o_ref[row0:row0 + rows, :] = x_ref[row0:row0 + rows, :] + jnp.dot(
        mix_scr[...], wout_ref[...], preferred_element_type=_F32)


def _mixer_kernel(x_ref, xn_ref, g_ref, win_ref, bgate_ref, convw_ref, convb_ref, mhn_ref, poolw_ref,
                  pscale_ref, wout_ref, o_ref, proj_a, proj_b, qk_scr, mix_scr, c_scr, n_scr, m_scr, *, chunk):
    rows = xn_ref.shape[0]
    step = pl.program_id(0)
    tile = functools.partial(
        _mix_tile, bgate_ref=bgate_ref, convw_ref=convw_ref, convb_ref=convb_ref, mhn_ref=mhn_ref,
        poolw_ref=poolw_ref, pscale_ref=pscale_ref, wout_ref=wout_ref, qk_scr=qk_scr, mix_scr=mix_scr,
        c_scr=c_scr, n_scr=n_scr, m_scr=m_scr, rows=rows, chunk=chunk)

    @pl.when(step == 0)
    def _():
        proj_a[0:HALO, :] = jnp.zeros((HALO, proj_a.shape[1]), _F32)
        c_scr[...] = jnp.zeros(c_scr.shape, _F32)
        n_scr[...] = jnp.zeros(n_scr.shape, _F32)
        m_scr[...] = jnp.zeros(m_scr.shape, _F32)
        _project(x_ref[0:rows, :], g_ref, win_ref, proj_a)

    _project(x_ref[rows:2 * rows, :], g_ref, win_ref, proj_b)
    proj_b[0:HALO, :] = proj_a[rows:rows + HALO, :]
    tile(proj_a, x_ref, o_ref, 0, step * (2 * rows))
    _project(xn_ref[...], g_ref, win_ref, proj_a)
    proj_a[0:HALO, :] = proj_b[rows:rows + HALO, :]
    tile(proj_b, x_ref, o_ref, rows, step * (2 * rows) + rows)


def _mixer_call(x, g, win, bgate, convw, convb, mhn, poolw, pscale, wout):
    s, d = x.shape
    rows, chunk = MIX_ROWS, MIX_CHUNK
    width = N_HEADS * HEAD_DIM
    cols = win.shape[1]
    assert s % (2 * rows) == 0 and rows % chunk == 0 and chunk % LANES == 0
    assert rows % CONV_BLOCK == 0 and rows % POOL_BLOCK == 0
    n_steps = s // (2 * rows)
    pair_spec = pl.BlockSpec((2 * rows, d), lambda i: (i, 0))
    next_spec = pl.BlockSpec((rows, d), lambda i: (jnp.minimum(2 * i + 2, 2 * n_steps - 1), 0))
    consts = [g, win, bgate, convw, convb, mhn, poolw, pscale, wout]
    in_specs = [pair_spec, next_spec] + [_const_spec(a.shape) for a in consts]
    vmem = (sum(a.size * a.dtype.itemsize for a in consts) + 10 * rows * d * 4
            + 2 * (rows + HALO) * cols * 4 + rows * 2 * width * 6 + 4 * rows * cols * 4)
    return pl.pallas_call(
        functools.partial(_mixer_kernel, chunk=chunk),
        grid=(n_steps,),
        in_specs=in_specs,
        out_specs=pair_spec,
        out_shape=jax.ShapeDtypeStruct((s, d), _F32),
        scratch_shapes=[
            pltpu.VMEM((rows + HALO, cols), _F32),
            pltpu.VMEM((rows + HALO, cols), _F32),
            pltpu.VMEM((rows, 2 * width), _F32),
            pltpu.VMEM((rows, 2 * width), _BF16),
            pltpu.VMEM((N_HEADS, HEAD_DIM, HEAD_DIM), _F32),
            pltpu.VMEM((N_HEADS, 1, HEAD_DIM), _F32),
            pltpu.VMEM((8, LANES), _F32),
        ],
        compiler_params=pltpu.CompilerParams(dimension_semantics=("arbitrary",),
                                             vmem_limit_bytes=_vmem_limit(vmem)),
        name="mixer",
    )(x, x, *consts)


def kernel(x, ffn1_norm, ffn1_w_gate, ffn1_w_up, ffn1_w_down, mix_norm, w_in, b_gates, conv_w, conv_b,
           mh_norm, pool_w, pool_scale, w_out, ffn2_norm, ffn2_w_gate, ffn2_w_up, ffn2_w_down, final_norm):
    bsz, s, d = x.shape
    depth = ffn1_norm.shape[0]
    assert depth >= 1
    width = N_HEADS * HEAD_DIM
    off_g = 4 * width
    off_p = off_g + 2 * N_HEADS
    bf = lambda a: a.astype(_BF16)
    outs = []
    for b in range(bsz):
        xb = x[b]
        for l in range(depth):
            xb = _ffn_call(xb, ffn1_norm[l][None, :], ffn1_w_gate[l], ffn1_w_up[l], ffn1_w_down[l])
            win = jnp.concatenate(
                [w_in[l][:, :off_g], w_in[l][:, off_p:],
                 jnp.pad(w_in[l][:, off_g:off_p], ((0, 0), (0, LANES - 2 * N_HEADS)))], axis=1)
            bgate = jnp.pad(b_gates[l], (0, LANES - 2 * N_HEADS))[None, :]
            xb = _mixer_call(
                xb, mix_norm[l][None, :], bf(win), bgate, conv_w[l], conv_b[l][None, :], mh_norm[l][None, :],
                bf(pool_w[l]), pool_scale[l][None, :], bf(w_out[l]))
            last = l == depth - 1
            xb = _ffn_call(xb, ffn2_norm[l][None, :], ffn2_w_gate[l], ffn2_w_up[l], ffn2_w_down[l],
                           final_g=final_norm[None, :] if last else None)
        outs.append(xb)
    return jnp.stack(outs, axis=0)
```

```python
import functools

import jax
import jax.numpy as jnp
from jax import lax
from jax.experimental import pallas as pl
from jax.experimental.pallas import tpu as pltpu

EPS = 1e-6
N_HEADS = 4
HEAD_DIM = 128
POOL_WINDOWS = (2, 4, 8, 16)
POOL_GROUP_DIM = 128
CONV_WIDTH = 4

LANES = 128
HALO = 16
V7X_VMEM_BYTES = 64 * 1024 * 1024

FFN_ROWS = 512
FFN_SUBTILES = 2
FFN_COLS = 256
WEIGHT_STAGE_SPLIT = 4
MIX_ROWS = 512
MIX_CHUNK = 256
CONV_BLOCK = 32
POOL_BLOCK = 128

_F32 = jnp.float32
_BF16 = jnp.bfloat16


def _rmsnorm(x, g):
    y = x * lax.rsqrt(jnp.mean(x * x, axis=-1, keepdims=True) + EPS)
    return y * g


def _const_spec(shape):
    return pl.BlockSpec(shape, lambda i: (0,) * len(shape), pipeline_mode=pl.Buffered(1))


def _vmem_limit(nbytes):
    return int(min(nbytes * 1.25 + (8 << 20), V7X_VMEM_BYTES - (4 << 20)))


def _stage_weights(jobs):
    copies = [pltpu.make_async_copy(src, stage, sem) for src, stage, sem, _ in jobs]
    copies[0].start()
    for i, (_, stage, _, dst) in enumerate(jobs):
        if i + 1 < len(jobs):
            copies[i + 1].start()
        copies[i].wait()
        dst[...] = stage[...].astype(_BF16)


def _ffn_kernel(x_ref, g_ref, wg_hbm, wu_hbm, wd_hbm, *rest, cols, final):
    if final:
        fg_ref, o_ref, h_scr, act_scr, wg_ref, wu_ref, wd_ref, stage_in, stage_out, sems = rest
    else:
        o_ref, h_scr, act_scr, wg_ref, wu_ref, wd_ref, stage_in, stage_out, sems = rest
    d, d_ff = wg_ref.shape

    @pl.when(pl.program_id(0) == 0)
    def _():
        jobs = []
        for w_hbm, w_ref, stage, sem0 in ((wg_hbm, wg_ref, stage_in, 0), (wu_hbm, wu_ref, stage_in, 0),
                                          (wd_hbm, wd_ref, stage_out, 2)):
            chunk = stage.shape[1]
            for c in range(w_hbm.shape[0] // chunk):
                rows = pl.ds(c * chunk, chunk)
                slot = len(jobs) % 2
                jobs.append((w_hbm.at[rows], stage.at[slot], sems.at[sem0 + slot], w_ref.at[rows]))
        _stage_weights(jobs)

    n_sub, rows = h_scr.shape[0], h_scr.shape[1]
    for t in range(n_sub):
        h_scr[t] = _rmsnorm(x_ref[t * rows:(t + 1) * rows, :], g_ref[...]).astype(_BF16)
    for t in range(n_sub):
        for j in range(d_ff // cols):
            sl = slice(j * cols, (j + 1) * cols)
            h = h_scr[t]
            gate = jnp.dot(h, wg_ref[:, sl], preferred_element_type=_F32)
            up = jnp.dot(h, wu_ref[:, sl], preferred_element_type=_F32)
            act_scr[t, :, sl] = (gate * jax.nn.sigmoid(gate) * up).astype(_BF16)
        down = jnp.dot(act_scr[t], wd_ref[...], preferred_element_type=_F32)
        y = x_ref[t * rows:(t + 1) * rows, :] + 0.5 * down
        if final:
            y = _rmsnorm(y, fg_ref[...])
        o_ref[t * rows:(t + 1) * rows, :] = y


def _ffn_call(x, g, wg, wu, wd, final_g=None):
    s, d = x.shape
    d_ff = wg.shape[1]
    rows, cols, n_sub = FFN_ROWS, FFN_COLS, FFN_SUBTILES
    block = n_sub * rows
    in_chunk, out_chunk = d // WEIGHT_STAGE_SPLIT, d_ff // WEIGHT_STAGE_SPLIT
    assert s % block == 0 and d_ff % cols == 0
    assert d % in_chunk == 0 and d_ff % out_chunk == 0 and in_chunk % 16 == 0 and out_chunk % 16 == 0
    assert WEIGHT_STAGE_SPLIT % 2 == 0
    final = final_g is not None
    row_spec = pl.BlockSpec((block, d), lambda i: (i, 0))
    hbm_spec = pl.BlockSpec(memory_space=pl.ANY)
    in_specs = [row_spec, _const_spec((1, d)), hbm_spec, hbm_spec, hbm_spec]
    args = [x, g, wg, wu, wd]
    if final:
        in_specs.append(_const_spec((1, d)))
        args.append(final_g)
    vmem = (3 * d * d_ff * 2 + 2 * (in_chunk * d_ff + out_chunk * d) * 4 + 4 * block * d * 4 + block * d * 2
            + block * d_ff * 2 + 4 * rows * cols * 4)
    return pl.pallas_call(
        functools.partial(_ffn_kernel, cols=cols, final=final),
        grid=(s // block,),
        in_specs=in_specs,
        out_specs=row_spec,
        out_shape=jax.ShapeDtypeStruct((s, d), _F32),
        scratch_shapes=[
            pltpu.VMEM((n_sub, rows, d), _BF16), pltpu.VMEM((n_sub, rows, d_ff), _BF16),
            pltpu.VMEM((d, d_ff), _BF16), pltpu.VMEM((d, d_ff), _BF16), pltpu.VMEM((d_ff, d), _BF16),
            pltpu.VMEM((2, in_chunk, d_ff), _F32),
            pltpu.VMEM((2, out_chunk, d), _F32),
            pltpu.SemaphoreType.DMA((4,))],
        compiler_params=pltpu.CompilerParams(dimension_semantics=("arbitrary",),
                                             vmem_limit_bytes=_vmem_limit(vmem)),
        name="swiglu_final" if final else "swiglu",
    )(*args)


def _gate_terms(gates, m_in, chunk):
    rows = gates.shape[0]
    g8 = gates.T[0:2 * N_HEADS, :]
    ig = g8[0:N_HEADS, :]
    t_in_chunk = lax.broadcasted_iota(jnp.int32, ig.shape, 1) % chunk
    b = jax.nn.log_sigmoid(g8[N_HEADS:2 * N_HEADS, :])
    shift = 1
    while shift < chunk:
        b = b + jnp.where(t_in_chunk >= shift, pltpu.roll(b, shift, axis=1), 0.0)
        shift *= 2
    drow = ig - b
    cm = drow
    shift = 1
    while shift < chunk:
        cm = jnp.maximum(cm, jnp.where(t_in_chunk >= shift, pltpu.roll(cm, shift, axis=1), -jnp.inf))
        shift *= 2
    neg_mx, inter_w, exp_neg_m, w_state, decays = [], [], [], [], []
    m_prev = m_in
    for c in range(rows // chunk):
        sl = slice(c * chunk, (c + 1) * chunk)
        g = b[:, (c + 1) * chunk - 1:(c + 1) * chunk]
        a = g + drow[:, sl]
        m_new = jnp.maximum(g + m_prev, jnp.max(a, axis=1, keepdims=True))
        mx = jnp.maximum(m_prev, cm[:, sl])
        neg_mx.append(-mx)
        inter_w.append(jnp.exp(m_prev - mx))
        exp_neg_m.append(jnp.exp(-(b[:, sl] + mx)))
        w_state.append(jnp.exp(a - m_new))
        decays.append(jnp.exp(g + m_prev - m_new))
        m_prev = m_new
    cat = lambda parts: jnp.concatenate(parts, axis=1)
    stacked = jnp.concatenate(
        [cat(neg_mx), cat(inter_w), cat(exp_neg_m), cat(w_state), jnp.zeros((LANES - 4 * N_HEADS, rows), _F32)],
        axis=0)
    return stacked.T, drow, decays, m_prev


def _mlstm_chunk(q, k, v, neg_mx, inter_w, exp_neg_m, w_state, d_row, decay, c_prev, n_prev):
    length = q.shape[0]
    qb, kb, vb = q.astype(_BF16), k.astype(_BF16), v.astype(_BF16)
    row_id = lax.broadcasted_iota(jnp.int32, (length, length), 0)
    col_id = lax.broadcasted_iota(jnp.int32, (length, length), 1)
    dw = jnp.exp(jnp.where(row_id >= col_id, neg_mx + d_row, -jnp.inf))
    scores = lax.dot_general(qb, kb, (((1,), (1,)), ((), ())), preferred_element_type=_F32) * dw
    num = (jnp.dot(scores.astype(_BF16), vb, preferred_element_type=_F32)
           + inter_w * jnp.dot(qb, c_prev.astype(_BF16), preferred_element_type=_F32))
    qn = (jnp.sum(scores, axis=1, keepdims=True)
          + inter_w * jnp.sum(q * n_prev, axis=1, keepdims=True))
    h = num / jnp.maximum(jnp.abs(qn), exp_neg_m)
    kw = k * w_state
    c_new = decay * c_prev + lax.dot_general(kw.astype(_BF16), vb, (((0,), (0,)), ((), ())),
                                             preferred_element_type=_F32)
    n_new = decay * n_prev + jnp.sum(kw, axis=0, keepdims=True)
    return h, c_new, n_new


def _project(x, g_ref, win_ref, proj_ref, qkp_ref):
    hb = _rmsnorm(x, g_ref[...]).astype(_BF16)
    n_slab = qkp_ref.shape[0]
    for c in range(0, n_slab, 2):
        res = jnp.dot(hb, win_ref[:, c * LANES:(c + 2) * LANES], preferred_element_type=_F32)
        qkp_ref[c, HALO:, :] = res[:, 0:LANES]
        qkp_ref[c + 1, HALO:, :] = res[:, LANES:2 * LANES]
    proj_ref[HALO:, :] = jnp.dot(hb, win_ref[:, n_slab * LANES:], preferred_element_type=_F32)


def _conv_silu(qkp_ref, qk_scr, convw_ref, convb_ref, rows):
    stride = CONV_BLOCK // 8
    for c in range(qkp_ref.shape[0]):
        cs = slice(c * LANES, (c + 1) * LANES)
        taps = [jnp.broadcast_to(convw_ref[j:j + 1, cs], (8, LANES)) for j in range(CONV_WIDTH)]
        bias = jnp.broadcast_to(convb_ref[:, cs], (8, LANES))
        for rb in range(rows // CONV_BLOCK):
            base = HALO + rb * CONV_BLOCK
            regs = [qkp_ref[c, pl.ds(base - (CONV_WIDTH - 1) + i, 8, stride=stride), :]
                    for i in range(stride + CONV_WIDTH - 1)]
            for m in range(stride):
                conv = bias
                for j in range(CONV_WIDTH):
                    conv = conv + taps[j] * regs[m + j]
                qk_scr[c, pl.ds(rb * CONV_BLOCK + m, 8, stride=stride), :] = conv * jax.nn.sigmoid(conv)


def _mix_tile(proj_ref, qkp_ref, x_ref, o_ref, row0, first_pos, *, bgate_ref, convw_ref, convb_ref, mhn_ref,
              poolw_ref, pscale_ref, wout_ref, qk_scr, mix_scr, c_scr, n_scr, m_scr, rows, chunk):
    width = N_HEADS * HEAD_DIM
    off_pool = 2 * width
    off_gate = off_pool + len(POOL_WINDOWS) * POOL_GROUP_DIM

    _conv_silu(qkp_ref, qk_scr, convw_ref, convb_ref, rows)

    gates = proj_ref[HALO:, off_gate:off_gate + LANES] + bgate_ref[...]
    cols, drow, decays, m_out = _gate_terms(gates, m_scr[0:N_HEADS, 0:1], chunk)
    m_scr[0:N_HEADS, 0:1] = m_out

    for c in range(rows // chunk):
        rs = slice(c * chunk, (c + 1) * chunk)
        rh = slice(HALO + c * chunk, HALO + (c + 1) * chunk)
        for hd in range(N_HEADS):
            hs = slice(hd * HEAD_DIM, (hd + 1) * HEAD_DIM)
            q = qk_scr[hd, rs, :]
            k = qk_scr[N_HEADS + hd, rs, :] * (HEAD_DIM ** -0.5)
            v = proj_ref[rh, hd * HEAD_DIM:(hd + 1) * HEAD_DIM]
            og = proj_ref[rh, width + hd * HEAD_DIM: width + (hd + 1) * HEAD_DIM]
            h, c_new, n_new = _mlstm_chunk(
                q, k, v,
                cols[rs, hd:hd + 1], cols[rs, N_HEADS + hd:N_HEADS + hd + 1],
                cols[rs, 2 * N_HEADS + hd:2 * N_HEADS + hd + 1], cols[rs, 3 * N_HEADS + hd:3 * N_HEADS + hd + 1],
                drow[hd:hd + 1, rs], decays[c][hd:hd + 1, :], c_scr[hd], n_scr[hd])
            c_scr[hd] = c_new
            n_scr[hd] = n_new
            h = h * lax.rsqrt(jnp.mean(h * h, axis=-1, keepdims=True) + EPS)
            mix_scr[rs, hs] = (jax.nn.sigmoid(og) * (h * mhn_ref[:, hs])).astype(_BF16)

    for rb in range(rows // POOL_BLOCK):
        r0 = rb * POOL_BLOCK
        pos = first_pos + r0 + lax.broadcasted_iota(jnp.int32, (POOL_BLOCK, 1), 0) + 1
        for gi, win in enumerate(POOL_WINDOWS):
            cs = slice(off_pool + gi * POOL_GROUP_DIM, off_pool + (gi + 1) * POOL_GROUP_DIM)
            total = proj_ref[r0:r0 + HALO + POOL_BLOCK, cs]
            span = 1
            while span < win:
                total = total + pltpu.roll(total, span, axis=0)
                span *= 2
            inv_count = 1.0 / jnp.minimum(pos, win).astype(_F32)
            pooled = total[HALO:, :] * inv_count - proj_ref[HALO + r0:HALO + r0 + POOL_BLOCK, cs]
            mixed = jnp.dot(pooled.astype(_BF16), poolw_ref[gi], preferred_element_type=_F32)
            mix_scr[r0:r0 + POOL_BLOCK, width + gi * POOL_GROUP_DIM: width + (gi + 1) * POOL_GROUP_DIM] = (
                mixed * pscale_ref[:, gi * POOL_GROUP_DIM:(gi + 1) * POOL_GROUP_DIM]).astype(_BF16)

    o_ref[row0:row0 + rows, :] = x_ref[row0:row0 + rows, :] + jnp.dot(
        mix_scr[...], wout_ref[...], preferred_element_type=_F32)


def _mixer_kernel(x_ref, xn_ref, g_ref, win_ref, bgate_ref, convw_ref, convb_ref, mhn_ref, poolw_ref,
                  pscale_ref, wout_ref, o_ref, proj_a, proj_b, qkp_a, qkp_b, qk_scr, mix_scr, c_scr, n_scr, m_scr, *, chunk):
    rows = xn_ref.shape[0]
    step = pl.program_id(0)
    tile = functools.partial(
        _mix_tile, bgate_ref=bgate_ref, convw_ref=convw_ref, convb_ref=convb_ref, mhn_ref=mhn_ref,
        poolw_ref=poolw_ref, pscale_ref=pscale_ref, wout_ref=wout_ref, qk_scr=qk_scr, mix_scr=mix_scr,
        c_scr=c_scr, n_scr=n_scr, m_scr=m_scr, rows=rows, chunk=chunk)

    @pl.when(step == 0)
    def _():
        proj_a[0:HALO, :] = jnp.zeros((HALO, proj_a.shape[1]), _F32)
        qkp_a[:, 0:HALO, :] = jnp.zeros((qkp_a.shape[0], HALO, LANES), _F32)
        c_scr[...] = jnp.zeros(c_scr.shape, _F32)
        n_scr[...] = jnp.zeros(n_scr.shape, _F32)
        m_scr[...] = jnp.zeros(m_scr.shape, _F32)
        _project(x_ref[0:rows, :], g_ref, win_ref, proj_a, qkp_a)

    _project(x_ref[rows:2 * rows, :], g_ref, win_ref, proj_b, qkp_b)
    proj_b[0:HALO, :] = proj_a[rows:rows + HALO, :]
    qkp_b[:, 0:HALO, :] = qkp_a[:, rows:rows + HALO, :]
    tile(proj_a, qkp_a, x_ref, o_ref, 0, step * (2 * rows))
    _project(xn_ref[...], g_ref, win_ref, proj_a, qkp_a)
    proj_a[0:HALO, :] = proj_b[rows:rows + HALO, :]
    qkp_a[:, 0:HALO, :] = qkp_b[:, rows:rows + HALO, :]
    tile(proj_b, qkp_b, x_ref, o_ref, rows, step * (2 * rows) + rows)


def _mixer_call(x, g, win, bgate, convw, convb, mhn, poolw, pscale, wout):
    s, d = x.shape
    rows, chunk = MIX_ROWS, MIX_CHUNK
    width = N_HEADS * HEAD_DIM
    cols = win.shape[1]
    assert s % (2 * rows) == 0 and rows % chunk == 0 and chunk % LANES == 0
    assert rows % CONV_BLOCK == 0 and rows % POOL_BLOCK == 0 and CONV_BLOCK % 8 == 0 and HEAD_DIM == LANES
    n_steps = s // (2 * rows)
    pair_spec = pl.BlockSpec((2 * rows, d), lambda i: (i, 0))
    next_spec = pl.BlockSpec((rows, d), lambda i: (jnp.minimum(2 * i + 2, 2 * n_steps - 1), 0))
    consts = [g, win, bgate, convw, convb, mhn, poolw, pscale, wout]
    in_specs = [pair_spec, next_spec] + [_const_spec(a.shape) for a in consts]
    vmem = (sum(a.size * a.dtype.itemsize for a in consts) + 10 * rows * d * 4
            + 2 * (rows + HALO) * cols * 4 + rows * 2 * width * 6 + 4 * rows * cols * 4)
    return pl.pallas_call(
        functools.partial(_mixer_kernel, chunk=chunk),
        grid=(n_steps,),
        in_specs=in_specs,
        out_specs=pair_spec,
        out_shape=jax.ShapeDtypeStruct((s, d), _F32),
        scratch_shapes=[
            pltpu.VMEM((rows + HALO, cols - 2 * width), _F32),
            pltpu.VMEM((rows + HALO, cols - 2 * width), _F32),
            pltpu.VMEM((2 * N_HEADS, rows + HALO, LANES), _F32),
            pltpu.VMEM((2 * N_HEADS, rows + HALO, LANES), _F32),
            pltpu.VMEM((2 * N_HEADS, rows, LANES), _F32),
            pltpu.VMEM((rows, 2 * width), _BF16),
            pltpu.VMEM((N_HEADS, HEAD_DIM, HEAD_DIM), _F32),
            pltpu.VMEM((N_HEADS, 1, HEAD_DIM), _F32),
            pltpu.VMEM((8, LANES), _F32),
        ],
        compiler_params=pltpu.CompilerParams(dimension_semantics=("arbitrary",),
                                             vmem_limit_bytes=_vmem_limit(vmem)),
        name="mixer",
    )(x, x, *consts)


def kernel(x, ffn1_norm, ffn1_w_gate, ffn1_w_up, ffn1_w_down, mix_norm, w_in, b_gates, conv_w, conv_b,
           mh_norm, pool_w, pool_scale, w_out, ffn2_norm, ffn2_w_gate, ffn2_w_up, ffn2_w_down, final_norm):
    bsz, s, d = x.shape
    depth = ffn1_norm.shape[0]
    assert depth >= 1
    width = N_HEADS * HEAD_DIM
    off_g = 4 * width
    off_p = off_g + 2 * N_HEADS
    bf = lambda a: a.astype(_BF16)
    outs = []
    for b in range(bsz):
        xb = x[b]
        for l in range(depth):
            xb = _ffn_call(xb, ffn1_norm[l][None, :], ffn1_w_gate[l], ffn1_w_up[l], ffn1_w_down[l])
            win = jnp.concatenate(
                [w_in[l][:, :off_g], w_in[l][:, off_p:],
                 jnp.pad(w_in[l][:, off_g:off_p], ((0, 0), (0, LANES - 2 * N_HEADS)))], axis=1)
            bgate = jnp.pad(b_gates[l], (0, LANES - 2 * N_HEADS))[None, :]
            xb = _mixer_call(
                xb, mix_norm[l][None, :], bf(win), bgate, conv_w[l], conv_b[l][None, :], mh_norm[l][None, :],
                bf(pool_w[l]), pool_scale[l][None, :], bf(w_out[l]))
            last = l == depth - 1
            xb = _ffn_call(xb, ffn2_norm[l][None, :], ffn2_w_gate[l], ffn2_w_up[l], ffn2_w_down[l],
                           final_g=final_norm[None, :] if last else None)
        outs.append(xb)
    return jnp.stack(outs, axis=0)
```

```python
import functools

import jax
import jax.numpy as jnp
from jax import lax
from jax.experimental import pallas as pl
from jax.experimental.pallas import tpu as pltpu

EPS = 1e-6
N_HEADS = 4
HEAD_DIM = 128
POOL_WINDOWS = (2, 4, 8, 16)
POOL_GROUP_DIM = 128
CONV_WIDTH = 4

LANES = 128
HALO = 16
V7X_VMEM_BYTES = 64 * 1024 * 1024

FFN_ROWS = 512
FFN_SUBTILES = 2
FFN_COLS = 256
WEIGHT_STAGE_SPLIT = 8
WEIGHT_STAGE_DEPTH = 4
MIX_ROWS = 512
MIX_CHUNK = 256
CONV_BLOCK = 32
POOL_BLOCK = 128

_F32 = jnp.float32
_BF16 = jnp.bfloat16


def _rmsnorm(x, g):
    y = x * lax.rsqrt(jnp.mean(x * x, axis=-1, keepdims=True) + EPS)
    return y * g


def _const_spec(shape):
    return pl.BlockSpec(shape, lambda i: (0,) * len(shape), pipeline_mode=pl.Buffered(1))


def _vmem_limit(nbytes):
    return int(min(nbytes * 1.25 + (8 << 20), V7X_VMEM_BYTES - (4 << 20)))


def _stage_weights(jobs, depth):
    copies = [pltpu.make_async_copy(src, stage, sem) for src, stage, sem, _ in jobs]
    for i in range(min(depth - 1, len(jobs))):
        copies[i].start()
    for i, (_, stage, _, dst) in enumerate(jobs):
        if i + depth - 1 < len(jobs):
            copies[i + depth - 1].start()
        copies[i].wait()
        dst[...] = stage[...].astype(_BF16)


def _ffn_kernel(x_ref, g_ref, wg_hbm, wu_hbm, wd_hbm, *rest, cols, final):
    if final:
        fg_ref, o_ref, h_scr, act_scr, wg_ref, wu_ref, wd_ref, stage_in, stage_out, sems = rest
    else:
        o_ref, h_scr, act_scr, wg_ref, wu_ref, wd_ref, stage_in, stage_out, sems = rest
    d, d_ff = wg_ref.shape

    @pl.when(pl.program_id(0) == 0)
    def _():
        jobs = []
        depth = stage_in.shape[0]
        for w_hbm, w_ref, stage, sem0 in ((wg_hbm, wg_ref, stage_in, 0), (wu_hbm, wu_ref, stage_in, 0),
                                          (wd_hbm, wd_ref, stage_out, depth)):
            chunk = stage.shape[1]
            for c in range(w_hbm.shape[0] // chunk):
                rows = pl.ds(c * chunk, chunk)
                slot = len(jobs) % depth
                jobs.append((w_hbm.at[rows], stage.at[slot], sems.at[sem0 + slot], w_ref.at[rows]))
        _stage_weights(jobs, depth)

    n_sub, rows = h_scr.shape[0], h_scr.shape[1]
    for t in range(n_sub):
        h_scr[t] = _rmsnorm(x_ref[t * rows:(t + 1) * rows, :], g_ref[...]).astype(_BF16)
    for t in range(n_sub):
        for j in range(d_ff // cols):
            sl = slice(j * cols, (j + 1) * cols)
            h = h_scr[t]
            gate = jnp.dot(h, wg_ref[:, sl], preferred_element_type=_F32)
            up = jnp.dot(h, wu_ref[:, sl], preferred_element_type=_F32)
            act_scr[t, :, sl] = (gate * jax.nn.sigmoid(gate) * up).astype(_BF16)
        down = jnp.dot(act_scr[t], wd_ref[...], preferred_element_type=_F32)
        y = x_ref[t * rows:(t + 1) * rows, :] + 0.5 * down
        if final:
            y = _rmsnorm(y, fg_ref[...])
        o_ref[t * rows:(t + 1) * rows, :] = y


def _ffn_call(x, g, wg, wu, wd, final_g=None):
    s, d = x.shape
    d_ff = wg.shape[1]
    rows, cols, n_sub = FFN_ROWS, FFN_COLS, FFN_SUBTILES
    block = n_sub * rows
    in_chunk, out_chunk = d // WEIGHT_STAGE_SPLIT, d_ff // WEIGHT_STAGE_SPLIT
    assert s % block == 0 and d_ff % cols == 0
    assert d % in_chunk == 0 and d_ff % out_chunk == 0 and in_chunk % 16 == 0 and out_chunk % 16 == 0
    assert WEIGHT_STAGE_SPLIT % WEIGHT_STAGE_DEPTH == 0
    final = final_g is not None
    row_spec = pl.BlockSpec((block, d), lambda i: (i, 0))
    hbm_spec = pl.BlockSpec(memory_space=pl.ANY)
    in_specs = [row_spec, _const_spec((1, d)), hbm_spec, hbm_spec, hbm_spec]
    args = [x, g, wg, wu, wd]
    if final:
        in_specs.append(_const_spec((1, d)))
        args.append(final_g)
    vmem = (3 * d * d_ff * 2 + WEIGHT_STAGE_DEPTH * (in_chunk * d_ff + out_chunk * d) * 4 + 4 * block * d * 4 + block * d * 2
            + block * d_ff * 2 + 4 * rows * cols * 4)
    return pl.pallas_call(
        functools.partial(_ffn_kernel, cols=cols, final=final),
        grid=(s // block,),
        in_specs=in_specs,
        out_specs=row_spec,
        out_shape=jax.ShapeDtypeStruct((s, d), _F32),
        scratch_shapes=[
            pltpu.VMEM((n_sub, rows, d), _BF16), pltpu.VMEM((n_sub, rows, d_ff), _BF16),
            pltpu.VMEM((d, d_ff), _BF16), pltpu.VMEM((d, d_ff), _BF16), pltpu.VMEM((d_ff, d), _BF16),
            pltpu.VMEM((WEIGHT_STAGE_DEPTH, in_chunk, d_ff), _F32),
            pltpu.VMEM((WEIGHT_STAGE_DEPTH, out_chunk, d), _F32),
            pltpu.SemaphoreType.DMA((2 * WEIGHT_STAGE_DEPTH,))],
        compiler_params=pltpu.CompilerParams(dimension_semantics=("arbitrary",),
                                             vmem_limit_bytes=_vmem_limit(vmem)),
        name="swiglu_final" if final else "swiglu",
    )(*args)


def _gate_terms(gates, m_in, chunk):
    rows = gates.shape[0]
    g8 = gates.T[0:2 * N_HEADS, :]
    ig = g8[0:N_HEADS, :]
    t_in_chunk = lax.broadcasted_iota(jnp.int32, ig.shape, 1) % chunk
    b = jax.nn.log_sigmoid(g8[N_HEADS:2 * N_HEADS, :])
    shift = 1
    while shift < chunk:
        b = b + jnp.where(t_in_chunk >= shift, pltpu.roll(b, shift, axis=1), 0.0)
        shift *= 2
    drow = ig - b
    cm = drow
    shift = 1
    while shift < chunk:
        cm = jnp.maximum(cm, jnp.where(t_in_chunk >= shift, pltpu.roll(cm, shift, axis=1), -jnp.inf))
        shift *= 2
    neg_mx, inter_w, exp_neg_m, w_state, decays = [], [], [], [], []
    m_prev = m_in
    for c in range(rows // chunk):
        sl = slice(c * chunk, (c + 1) * chunk)
        g = b[:, (c + 1) * chunk - 1:(c + 1) * chunk]
        a = g + drow[:, sl]
        m_new = jnp.maximum(g + m_prev, jnp.max(a, axis=1, keepdims=True))
        mx = jnp.maximum(m_prev, cm[:, sl])
        neg_mx.append(-mx)
        inter_w.append(jnp.exp(m_prev - mx))
        exp_neg_m.append(jnp.exp(-(b[:, sl] + mx)))
        w_state.append(jnp.exp(a - m_new))
        decays.append(jnp.exp(g + m_prev - m_new))
        m_prev = m_new
    cat = lambda parts: jnp.concatenate(parts, axis=1)
    stacked = jnp.concatenate(
        [cat(neg_mx), cat(inter_w), cat(exp_neg_m), cat(w_state), jnp.zeros((LANES - 4 * N_HEADS, rows), _F32)],
        axis=0)
    return stacked.T, drow, decays, m_prev


def _mlstm_chunk(q, k, v, neg_mx, inter_w, exp_neg_m, w_state, d_row, decay, c_prev, n_prev):
    length = q.shape[0]
    qb, kb, vb = q.astype(_BF16), k.astype(_BF16), v.astype(_BF16)
    row_id = lax.broadcasted_iota(jnp.int32, (length, length), 0)
    col_id = lax.broadcasted_iota(jnp.int32, (length, length), 1)
    dw = jnp.exp(jnp.where(row_id >= col_id, neg_mx + d_row, -jnp.inf))
    scores = lax.dot_general(qb, kb, (((1,), (1,)), ((), ())), preferred_element_type=_F32) * dw
    num = (jnp.dot(scores.astype(_BF16), vb, preferred_element_type=_F32)
           + inter_w * jnp.dot(qb, c_prev.astype(_BF16), preferred_element_type=_F32))
    qn = (jnp.sum(scores, axis=1, keepdims=True)
          + inter_w * jnp.sum(q * n_prev, axis=1, keepdims=True))
    h = num / jnp.maximum(jnp.abs(qn), exp_neg_m)
    kw = k * w_state
    c_new = decay * c_prev + lax.dot_general(kw.astype(_BF16), vb, (((0,), (0,)), ((), ())),
                                             preferred_element_type=_F32)
    n_new = decay * n_prev + jnp.sum(kw, axis=0, keepdims=True)
    return h, c_new, n_new


def _project(x, g_ref, win_ref, proj_ref):
    hb = _rmsnorm(x, g_ref[...]).astype(_BF16)
    proj_ref[HALO:, :] = jnp.dot(hb, win_ref[...], preferred_element_type=_F32)


def _mix_tile(proj_ref, x_ref, o_ref, row0, first_pos, *, bgate_ref, convw_ref, convb_ref, mhn_ref, poolw_ref,
              pscale_ref, wout_ref, qk_scr, mix_scr, c_scr, n_scr, m_scr, rows, chunk):
    width = N_HEADS * HEAD_DIM
    off_pool = 4 * width
    off_gate = off_pool + len(POOL_WINDOWS) * POOL_GROUP_DIM

    for rb in range(rows // CONV_BLOCK):
        base = HALO + rb * CONV_BLOCK
        conv = convb_ref[...] + convw_ref[CONV_WIDTH - 1:CONV_WIDTH, :] * proj_ref[base:base + CONV_BLOCK, 0:2 * width]
        for j in range(CONV_WIDTH - 1):
            off = base - (CONV_WIDTH - 1) + j
            conv = conv + convw_ref[j:j + 1, :] * proj_ref[off:off + CONV_BLOCK, 0:2 * width]
        qk_scr[rb * CONV_BLOCK:(rb + 1) * CONV_BLOCK, :] = conv * jax.nn.sigmoid(conv)

    gates = proj_ref[HALO:, off_gate:off_gate + LANES] + bgate_ref[...]
    cols, drow, decays, m_out = _gate_terms(gates, m_scr[0:N_HEADS, 0:1], chunk)
    m_scr[0:N_HEADS, 0:1] = m_out

    for c in range(rows // chunk):
        rs = slice(c * chunk, (c + 1) * chunk)
        rh = slice(HALO + c * chunk, HALO + (c + 1) * chunk)
        for hd in range(N_HEADS):
            hs = slice(hd * HEAD_DIM, (hd + 1) * HEAD_DIM)
            q = qk_scr[rs, hs]
            k = qk_scr[rs, width + hd * HEAD_DIM: width + (hd + 1) * HEAD_DIM] * (HEAD_DIM ** -0.5)
            v = proj_ref[rh, 2 * width + hd * HEAD_DIM: 2 * width + (hd + 1) * HEAD_DIM]
            og = proj_ref[rh, 3 * width + hd * HEAD_DIM: 3 * width + (hd + 1) * HEAD_DIM]
            h, c_new, n_new = _mlstm_chunk(
                q, k, v,
                cols[rs, hd:hd + 1], cols[rs, N_HEADS + hd:N_HEADS + hd + 1],
                cols[rs, 2 * N_HEADS + hd:2 * N_HEADS + hd + 1], cols[rs, 3 * N_HEADS + hd:3 * N_HEADS + hd + 1],
                drow[hd:hd + 1, rs], decays[c][hd:hd + 1, :], c_scr[hd], n_scr[hd])
            c_scr[hd] = c_new
            n_scr[hd] = n_new
            h = h * lax.rsqrt(jnp.mean(h * h, axis=-1, keepdims=True) + EPS)
            mix_scr[rs, hs] = (jax.nn.sigmoid(og) * (h * mhn_ref[:, hs])).astype(_BF16)

    for rb in range(rows // POOL_BLOCK):
        r0 = rb * POOL_BLOCK
        pos = first_pos + r0 + lax.broadcasted_iota(jnp.int32, (POOL_BLOCK, 1), 0) + 1
        for gi, win in enumerate(POOL_WINDOWS):
            cs = slice(off_pool + gi * POOL_GROUP_DIM, off_pool + (gi + 1) * POOL_GROUP_DIM)
            total = proj_ref[r0:r0 + HALO + POOL_BLOCK, cs]
            span = 1
            while span < win:
                total = total + pltpu.roll(total, span, axis=0)
                span *= 2
            inv_count = 1.0 / jnp.minimum(pos, win).astype(_F32)
            pooled = total[HALO:, :] * inv_count - proj_ref[HALO + r0:HALO + r0 + POOL_BLOCK, cs]
            mixed = jnp.dot(pooled.astype(_BF16), poolw_ref[gi], preferred_element_type=_F32)
            mix_scr[r0:r0 + POOL_BLOCK, width + gi * POOL_GROUP_DIM: width + (gi + 1) * POOL_GROUP_DIM] = (
                mixed * pscale_ref[:, gi * POOL_GROUP_DIM:(gi + 1) * POOL_GROUP_DIM]).astype(_BF16)

    o_ref[row0:row0 + rows, :] = x_ref[row0:row0 + rows, :] + jnp.dot(
        mix_scr[...], wout_ref[...], preferred_element_type=_F32)


def _mixer_kernel(x_ref, xn_ref, g_ref, win_ref, bgate_ref, convw_ref, convb_ref, mhn_ref, poolw_ref,
                  pscale_ref, wout_ref, o_ref, proj_a, proj_b, qk_scr, mix_scr, c_scr, n_scr, m_scr, *, chunk):
    rows = xn_ref.shape[0]
    step = pl.program_id(0)
    tile = functools.partial(
        _mix_tile, bgate_ref=bgate_ref, convw_ref=convw_ref, convb_ref=convb_ref, mhn_ref=mhn_ref,
        poolw_ref=poolw_ref, pscale_ref=pscale_ref, wout_ref=wout_ref, qk_scr=qk_scr, mix_scr=mix_scr,
        c_scr=c_scr, n_scr=n_scr, m_scr=m_scr, rows=rows, chunk=chunk)

    @pl.when(step == 0)
    def _():
        proj_a[0:HALO, :] = jnp.zeros((HALO, proj_a.shape[1]), _F32)
        c_scr[...] = jnp.zeros(c_scr.shape, _F32)
        n_scr[...] = jnp.zeros(n_scr.shape, _F32)
        m_scr[...] = jnp.zeros(m_scr.shape, _F32)
        _project(x_ref[0:rows, :], g_ref, win_ref, proj_a)

    _project(x_ref[rows:2 * rows, :], g_ref, win_ref, proj_b)
    proj_b[0:HALO, :] = proj_a[rows:rows + HALO, :]
    tile(proj_a, x_ref, o_ref, 0, step * (2 * rows))
    _project(xn_ref[...], g_ref, win_ref, proj_a)
    proj_a[0:HALO, :] = proj_b[rows:rows + HALO, :]
    tile(proj_b, x_ref, o_ref, rows, step * (2 * rows) + rows)


def _mixer_call(x, g, win, bgate, convw, convb, mhn, poolw, pscale, wout):
    s, d = x.shape
    rows, chunk = MIX_ROWS, MIX_CHUNK
    width = N_HEADS * HEAD_DIM
    cols = win.shape[1]
    assert s % (2 * rows) == 0 and rows % chunk == 0 and chunk % LANES == 0
    assert rows % CONV_BLOCK == 0 and rows % POOL_BLOCK == 0
    n_steps = s // (2 * rows)
    pair_spec = pl.BlockSpec((2 * rows, d), lambda i: (i, 0))
    next_spec = pl.BlockSpec((rows, d), lambda i: (jnp.minimum(2 * i + 2, 2 * n_steps - 1), 0))
    consts = [g, win, bgate, convw, convb, mhn, poolw, pscale, wout]
    in_specs = [pair_spec, next_spec] + [_const_spec(a.shape) for a in consts]
    vmem = (sum(a.size * a.dtype.itemsize for a in consts) + 10 * rows * d * 4
            + 2 * (rows + HALO) * cols * 4 + rows * 2 * width * 6 + 4 * rows * cols * 4)
    return pl.pallas_call(
        functools.partial(_mixer_kernel, chunk=chunk),
        grid=(n_steps,),
        in_specs=in_specs,
        out_specs=pair_spec,
        out_shape=jax.ShapeDtypeStruct((s, d), _F32),
        scratch_shapes=[
            pltpu.VMEM((rows + HALO, cols), _F32),
            pltpu.VMEM((rows + HALO, cols), _F32),
            pltpu.VMEM((rows, 2 * width), _F32),
            pltpu.VMEM((rows, 2 * width), _BF16),
            pltpu.VMEM((N_HEADS, HEAD_DIM, HEAD_DIM), _F32),
            pltpu.VMEM((N_HEADS, 1, HEAD_DIM), _F32),
            pltpu.VMEM((8, LANES), _F32),
        ],
        compiler_params=pltpu.CompilerParams(dimension_semantics=("arbitrary",),
                                             vmem_limit_bytes=_vmem_limit(vmem)),
        name="mixer",
    )(x, x, *consts)


def kernel(x, ffn1_norm, ffn1_w_gate, ffn1_w_up, ffn1_w_down, mix_norm, w_in, b_gates, conv_w, conv_b,
           mh_norm, pool_w, pool_scale, w_out, ffn2_norm, ffn2_w_gate, ffn2_w_up, ffn2_w_down, final_norm):
    bsz, s, d = x.shape
    depth = ffn1_norm.shape[0]
    assert depth >= 1
    width = N_HEADS * HEAD_DIM
    off_g = 4 * width
    off_p = off_g + 2 * N_HEADS
    bf = lambda a: a.astype(_BF16)
    outs = []
    for b in range(bsz):
        xb = x[b]
        for l in range(depth):
            xb = _ffn_call(xb, ffn1_norm[l][None, :], ffn1_w_gate[l], ffn1_w_up[l], ffn1_w_down[l])
            wb = bf(w_in[l])
            win = jnp.concatenate(
                [wb[:, :off_g], wb[:, off_p:], jnp.pad(wb[:, off_g:off_p], ((0, 0), (0, LANES - 2 * N_HEADS)))], axis=1)
            bgate = jnp.pad(b_gates[l], (0, LANES - 2 * N_HEADS))[None, :]
            xb = _mixer_call(
                xb, mix_norm[l][None, :], win, bgate, conv_w[l], conv_b[l][None, :], mh_norm[l][None, :],
                bf(pool_w[l]), pool_scale[l][None, :], bf(w_out[l]))
            last = l == depth - 1
            xb = _ffn_call(xb, ffn2_norm[l][None, :], ffn2_w_gate[l], ffn2_w_up[l], ffn2_w_down[l],
                           final_g=final_norm[None, :] if last else None)
        outs.append(xb)
    return jnp.stack(outs, axis=0)
```

```python
import functools

import jax
import jax.numpy as jnp
from jax import lax
from jax.experimental import pallas as pl
from jax.experimental.pallas import tpu as pltpu

EPS = 1e-6
N_HEADS = 4
HEAD_DIM = 128
POOL_WINDOWS = (2, 4, 8, 16)
POOL_GROUP_DIM = 128
CONV_WIDTH = 4

LANES = 128
HALO = 16
V7X_VMEM_BYTES = 64 * 1024 * 1024

FFN_ROWS = 512
FFN_SUBTILES = 2
FFN_COLS = 256
WEIGHT_STAGE_SPLIT = 8
WEIGHT_STAGE_DEPTH = 4
MIX_ROWS = 512
MIX_CHUNK = 256
CONV_BLOCK = 32
POOL_BLOCK = 128

_F32 = jnp.float32
_BF16 = jnp.bfloat16


def _rmsnorm(x, g):
    y = x * lax.rsqrt(jnp.mean(x * x, axis=-1, keepdims=True) + EPS)
    return y * g


def _const_spec(shape):
    return pl.BlockSpec(shape, lambda i: (0,) * len(shape), pipeline_mode=pl.Buffered(1))


def _vmem_limit(nbytes):
    return int(min(nbytes * 1.25 + (8 << 20), V7X_VMEM_BYTES - (4 << 20)))


def _stage_weights(jobs, depth):
    copies = [pltpu.make_async_copy(src, stage, sem) for src, stage, sem, _ in jobs]
    for i in range(min(depth - 1, len(jobs))):
        copies[i].start()
    for i, (_, stage, _, dst) in enumerate(jobs):
        if i + depth - 1 < len(jobs):
            copies[i + depth - 1].start()
        copies[i].wait()
        dst[...] = stage[...].astype(_BF16)


def _ffn_kernel(x_ref, g_ref, wg_hbm, wu_hbm, wd_hbm, *rest, cols, final, layer):
    if final:
        fg_ref, o_ref, h_scr, act_scr, wg_ref, wu_ref, wd_ref, stage_in, stage_out, sems = rest
    else:
        o_ref, h_scr, act_scr, wg_ref, wu_ref, wd_ref, stage_in, stage_out, sems = rest
    d, d_ff = wg_ref.shape

    @pl.when(pl.program_id(0) == 0)
    def _():
        jobs = []
        depth = stage_in.shape[0]
        for w_hbm, w_ref, stage, sem0 in ((wg_hbm, wg_ref, stage_in, 0), (wu_hbm, wu_ref, stage_in, 0),
                                          (wd_hbm, wd_ref, stage_out, depth)):
            chunk = stage.shape[1]
            for c in range(w_ref.shape[0] // chunk):
                rows = pl.ds(c * chunk, chunk)
                slot = len(jobs) % depth
                jobs.append((w_hbm.at[layer, rows], stage.at[slot], sems.at[sem0 + slot], w_ref.at[rows]))
        _stage_weights(jobs, depth)

    n_sub, rows = h_scr.shape[0], h_scr.shape[1]
    for t in range(n_sub):
        h_scr[t] = _rmsnorm(x_ref[t * rows:(t + 1) * rows, :], g_ref[...]).astype(_BF16)
    for t in range(n_sub):
        for j in range(d_ff // cols):
            sl = slice(j * cols, (j + 1) * cols)
            h = h_scr[t]
            gate = jnp.dot(h, wg_ref[:, sl], preferred_element_type=_F32)
            up = jnp.dot(h, wu_ref[:, sl], preferred_element_type=_F32)
            act_scr[t, :, sl] = (gate * jax.nn.sigmoid(gate) * up).astype(_BF16)
        down = jnp.dot(act_scr[t], wd_ref[...], preferred_element_type=_F32)
        y = x_ref[t * rows:(t + 1) * rows, :] + 0.5 * down
        if final:
            y = _rmsnorm(y, fg_ref[...])
        o_ref[t * rows:(t + 1) * rows, :] = y


def _ffn_call(x, g, wg, wu, wd, layer, final_g=None):
    s, d = x.shape
    d_ff = wg.shape[2]
    rows, cols, n_sub = FFN_ROWS, FFN_COLS, FFN_SUBTILES
    block = n_sub * rows
    in_chunk, out_chunk = d // WEIGHT_STAGE_SPLIT, d_ff // WEIGHT_STAGE_SPLIT
    assert s % block == 0 and d_ff % cols == 0
    assert d % in_chunk == 0 and d_ff % out_chunk == 0 and in_chunk % 16 == 0 and out_chunk % 16 == 0
    assert WEIGHT_STAGE_SPLIT % WEIGHT_STAGE_DEPTH == 0
    final = final_g is not None
    row_spec = pl.BlockSpec((block, d), lambda i: (i, 0))
    hbm_spec = pl.BlockSpec(memory_space=pl.ANY)
    in_specs = [row_spec, _const_spec((1, d)), hbm_spec, hbm_spec, hbm_spec]
    args = [x, g, wg, wu, wd]
    if final:
        in_specs.append(_const_spec((1, d)))
        args.append(final_g)
    vmem = (3 * d * d_ff * 2 + WEIGHT_STAGE_DEPTH * (in_chunk * d_ff + out_chunk * d) * 4 + 4 * block * d * 4 + block * d * 2
            + block * d_ff * 2 + 4 * rows * cols * 4)
    return pl.pallas_call(
        functools.partial(_ffn_kernel, cols=cols, final=final, layer=layer),
        grid=(s // block,),
        in_specs=in_specs,
        out_specs=row_spec,
        out_shape=jax.ShapeDtypeStruct((s, d), _F32),
        scratch_shapes=[
            pltpu.VMEM((n_sub, rows, d), _BF16), pltpu.VMEM((n_sub, rows, d_ff), _BF16),
            pltpu.VMEM((d, d_ff), _BF16), pltpu.VMEM((d, d_ff), _BF16), pltpu.VMEM((d_ff, d), _BF16),
            pltpu.VMEM((WEIGHT_STAGE_DEPTH, in_chunk, d_ff), _F32),
            pltpu.VMEM((WEIGHT_STAGE_DEPTH, out_chunk, d), _F32),
            pltpu.SemaphoreType.DMA((2 * WEIGHT_STAGE_DEPTH,))],
        compiler_params=pltpu.CompilerParams(dimension_semantics=("arbitrary",),
                                             vmem_limit_bytes=_vmem_limit(vmem)),
        name="swiglu_final" if final else "swiglu",
    )(*args)


def _gate_terms(gates, m_in, chunk):
    rows = gates.shape[0]
    g8 = gates.T[0:2 * N_HEADS, :]
    ig = g8[0:N_HEADS, :]
    t_in_chunk = lax.broadcasted_iota(jnp.int32, ig.shape, 1) % chunk
    b = jax.nn.log_sigmoid(g8[N_HEADS:2 * N_HEADS, :])
    shift = 1
    while shift < chunk:
        b = b + jnp.where(t_in_chunk >= shift, pltpu.roll(b, shift, axis=1), 0.0)
        shift *= 2
    drow = ig - b
    cm = drow
    shift = 1
    while shift < chunk:
        cm = jnp.maximum(cm, jnp.where(t_in_chunk >= shift, pltpu.roll(cm, shift, axis=1), -jnp.inf))
        shift *= 2
    neg_mx, inter_w, exp_neg_m, w_state, decays = [], [], [], [], []
    m_prev = m_in
    for c in range(rows // chunk):
        sl = slice(c * chunk, (c + 1) * chunk)
        g = b[:, (c + 1) * chunk - 1:(c + 1) * chunk]
        a = g + drow[:, sl]
        m_new = jnp.maximum(g + m_prev, jnp.max(a, axis=1, keepdims=True))
        mx = jnp.maximum(m_prev, cm[:, sl])
        neg_mx.append(-mx)
        inter_w.append(jnp.exp(m_prev - mx))
        exp_neg_m.append(jnp.exp(-(b[:, sl] + mx)))
        w_state.append(jnp.exp(a - m_new))
        decays.append(jnp.exp(g + m_prev - m_new))
        m_prev = m_new
    cat = lambda parts: jnp.concatenate(parts, axis=1)
    stacked = jnp.concatenate(
        [cat(neg_mx), cat(inter_w), cat(exp_neg_m), cat(w_state), jnp.zeros((LANES - 4 * N_HEADS, rows), _F32)],
        axis=0)
    return stacked.T, drow, decays, m_prev


def _mlstm_chunk(q, k, v, neg_mx, inter_w, exp_neg_m, w_state, d_row, decay, c_prev, n_prev):
    length = q.shape[0]
    qb, kb, vb = q.astype(_BF16), k.astype(_BF16), v.astype(_BF16)
    row_id = lax.broadcasted_iota(jnp.int32, (length, length), 0)
    col_id = lax.broadcasted_iota(jnp.int32, (length, length), 1)
    dw = jnp.exp(jnp.where(row_id >= col_id, neg_mx + d_row, -jnp.inf))
    scores = lax.dot_general(qb, kb, (((1,), (1,)), ((), ())), preferred_element_type=_F32) * dw
    num = (jnp.dot(scores.astype(_BF16), vb, preferred_element_type=_F32)
           + inter_w * jnp.dot(qb, c_prev.astype(_BF16), preferred_element_type=_F32))
    qn = (jnp.sum(scores, axis=1, keepdims=True)
          + inter_w * jnp.sum(q * n_prev, axis=1, keepdims=True))
    h = num / jnp.maximum(jnp.abs(qn), exp_neg_m)
    kw = k * w_state
    c_new = decay * c_prev + lax.dot_general(kw.astype(_BF16), vb, (((0,), (0,)), ((), ())),
                                             preferred_element_type=_F32)
    n_new = decay * n_prev + jnp.sum(kw, axis=0, keepdims=True)
    return h, c_new, n_new


def _project(x, g_ref, win_ref, proj_ref):
    hb = _rmsnorm(x, g_ref[...]).astype(_BF16)
    proj_ref[HALO:, :] = jnp.dot(hb, win_ref[...], preferred_element_type=_F32)


def _mix_tile(proj_ref, x_ref, o_ref, row0, first_pos, *, bgate_ref, convw_ref, convb_ref, mhn_ref, poolw_ref,
              pscale_ref, wout_ref, qk_scr, mix_scr, c_scr, n_scr, m_scr, rows, chunk):
    width = N_HEADS * HEAD_DIM
    off_pool = 4 * width
    off_gate = off_pool + len(POOL_WINDOWS) * POOL_GROUP_DIM

    for rb in range(rows // CONV_BLOCK):
        base = HALO + rb * CONV_BLOCK
        conv = convb_ref[...] + convw_ref[CONV_WIDTH - 1:CONV_WIDTH, :] * proj_ref[base:base + CONV_BLOCK, 0:2 * width]
        for j in range(CONV_WIDTH - 1):
            off = base - (CONV_WIDTH - 1) + j
            conv = conv + convw_ref[j:j + 1, :] * proj_ref[off:off + CONV_BLOCK, 0:2 * width]
        qk_scr[rb * CONV_BLOCK:(rb + 1) * CONV_BLOCK, :] = conv * jax.nn.sigmoid(conv)

    gates = proj_ref[HALO:, off_gate:off_gate + LANES] + bgate_ref[...]
    cols, drow, decays, m_out = _gate_terms(gates, m_scr[0:N_HEADS, 0:1], chunk)
    m_scr[0:N_HEADS, 0:1] = m_out

    for c in range(rows // chunk):
        rs = slice(c * chunk, (c + 1) * chunk)
        rh = slice(HALO + c * chunk, HALO + (c + 1) * chunk)
        for hd in range(N_HEADS):
            hs = slice(hd * HEAD_DIM, (hd + 1) * HEAD_DIM)
            q = qk_scr[rs, hs]
            k = qk_scr[rs, width + hd * HEAD_DIM: width + (hd + 1) * HEAD_DIM] * (HEAD_DIM ** -0.5)
            v = proj_ref[rh, 2 * width + hd * HEAD_DIM: 2 * width + (hd + 1) * HEAD_DIM]
            og = proj_ref[rh, 3 * width + hd * HEAD_DIM: 3 * width + (hd + 1) * HEAD_DIM]
            h, c_new, n_new = _mlstm_chunk(
                q, k, v,
                cols[rs, hd:hd + 1], cols[rs, N_HEADS + hd:N_HEADS + hd + 1],
                cols[rs, 2 * N_HEADS + hd:2 * N_HEADS + hd + 1], cols[rs, 3 * N_HEADS + hd:3 * N_HEADS + hd + 1],
                drow[hd:hd + 1, rs], decays[c][hd:hd + 1, :], c_scr[hd], n_scr[hd])
            c_scr[hd] = c_new
            n_scr[hd] = n_new
            h = h * lax.rsqrt(jnp.mean(h * h, axis=-1, keepdims=True) + EPS)
            mix_scr[rs, hs] = (jax.nn.sigmoid(og) * (h * mhn_ref[:, hs])).astype(_BF16)

    for rb in range(rows // POOL_BLOCK):
        r0 = rb * POOL_BLOCK
        pos = first_pos + r0 + lax.broadcasted_iota(jnp.int32, (POOL_BLOCK, 1), 0) + 1
        for gi, win in enumerate(POOL_WINDOWS):
            cs = slice(off_pool + gi * POOL_GROUP_DIM, off_pool + (gi + 1) * POOL_GROUP_DIM)
            total = proj_ref[r0:r0 + HALO + POOL_BLOCK, cs]
            span = 1
            while span < win:
                total = total + pltpu.roll(total, span, axis=0)
                span *= 2
            inv_count = 1.0 / jnp.minimum(pos, win).astype(_F32)
            pooled = total[HALO:, :] * inv_count - proj_ref[HALO + r0:HALO + r0 + POOL_BLOCK, cs]
            mixed = jnp.dot(pooled.astype(_BF16), poolw_ref[gi], preferred_element_type=_F32)
            mix_scr[r0:r0 + POOL_BLOCK, width + gi * POOL_GROUP_DIM: width + (gi + 1) * POOL_GROUP_DIM] = (
                mixed * pscale_ref[:, gi * POOL_GROUP_DIM:(gi + 1) * POOL_GROUP_DIM]).astype(_BF16)

    o_ref[row0:row0 + rows, :] = x_ref[row0:row0 + rows, :] + jnp.dot(
        mix_scr[...], wout_ref[...], preferred_element_type=_F32)


def _mixer_kernel(x_ref, xn_ref, g_ref, win_ref, bgate_ref, convw_ref, convb_ref, mhn_ref, poolw_ref,
                  pscale_ref, wout_ref, o_ref, proj_a, proj_b, qk_scr, mix_scr, c_scr, n_scr, m_scr, *, chunk):
    rows = xn_ref.shape[0]
    step = pl.program_id(0)
    tile = functools.partial(
        _mix_tile, bgate_ref=bgate_ref, convw_ref=convw_ref, convb_ref=convb_ref, mhn_ref=mhn_ref,
        poolw_ref=poolw_ref, pscale_ref=pscale_ref, wout_ref=wout_ref, qk_scr=qk_scr, mix_scr=mix_scr,
        c_scr=c_scr, n_scr=n_scr, m_scr=m_scr, rows=rows, chunk=chunk)

    @pl.when(step == 0)
    def _():
        proj_a[0:HALO, :] = jnp.zeros((HALO, proj_a.shape[1]), _F32)
        c_scr[...] = jnp.zeros(c_scr.shape, _F32)
        n_scr[...] = jnp.zeros(n_scr.shape, _F32)
        m_scr[...] = jnp.zeros(m_scr.shape, _F32)
        _project(x_ref[0:rows, :], g_ref, win_ref, proj_a)

    _project(x_ref[rows:2 * rows, :], g_ref, win_ref, proj_b)
    proj_b[0:HALO, :] = proj_a[rows:rows + HALO, :]
    tile(proj_a, x_ref, o_ref, 0, step * (2 * rows))
    _project(xn_ref[...], g_ref, win_ref, proj_a)
    proj_a[0:HALO, :] = proj_b[rows:rows + HALO, :]
    tile(proj_b, x_ref, o_ref, rows, step * (2 * rows) + rows)


def _mixer_call(x, g, win, bgate, convw, convb, mhn, poolw, pscale, wout):
    s, d = x.shape
    rows, chunk = MIX_ROWS, MIX_CHUNK
    width = N_HEADS * HEAD_DIM
    cols = win.shape[1]
    assert s % (2 * rows) == 0 and rows % chunk == 0 and chunk % LANES == 0
    assert rows % CONV_BLOCK == 0 and rows % POOL_BLOCK == 0
    n_steps = s // (2 * rows)
    pair_spec = pl.BlockSpec((2 * rows, d), lambda i: (i, 0))
    next_spec = pl.BlockSpec((rows, d), lambda i: (jnp.minimum(2 * i + 2, 2 * n_steps - 1), 0))
    consts = [g, win, bgate, convw, convb, mhn, poolw, pscale, wout]
    in_specs = [pair_spec, next_spec] + [_const_spec(a.shape) for a in consts]
    vmem = (sum(a.size * a.dtype.itemsize for a in consts) + 10 * rows * d * 4
            + 2 * (rows + HALO) * cols * 4 + rows * 2 * width * 6 + 4 * rows * cols * 4)
    return pl.pallas_call(
        functools.partial(_mixer_kernel, chunk=chunk),
        grid=(n_steps,),
        in_specs=in_specs,
        out_specs=pair_spec,
        out_shape=jax.ShapeDtypeStruct((s, d), _F32),
        scratch_shapes=[
            pltpu.VMEM((rows + HALO, cols), _F32),
            pltpu.VMEM((rows + HALO, cols), _F32),
            pltpu.VMEM((rows, 2 * width), _F32),
            pltpu.VMEM((rows, 2 * width), _BF16),
            pltpu.VMEM((N_HEADS, HEAD_DIM, HEAD_DIM), _F32),
            pltpu.VMEM((N_HEADS, 1, HEAD_DIM), _F32),
            pltpu.VMEM((8, LANES), _F32),
        ],
        compiler_params=pltpu.CompilerParams(dimension_semantics=("arbitrary",),
                                             vmem_limit_bytes=_vmem_limit(vmem)),
        name="mixer",
    )(x, x, *consts)


def kernel(x, ffn1_norm, ffn1_w_gate, ffn1_w_up, ffn1_w_down, mix_norm, w_in, b_gates, conv_w, conv_b,
           mh_norm, pool_w, pool_scale, w_out, ffn2_norm, ffn2_w_gate, ffn2_w_up, ffn2_w_down, final_norm):
    bsz, s, d = x.shape
    depth = ffn1_norm.shape[0]
    assert depth >= 1
    width = N_HEADS * HEAD_DIM
    off_g = 4 * width
    off_p = off_g + 2 * N_HEADS
    bf = lambda a: a.astype(_BF16)
    outs = []
    for b in range(bsz):
        xb = x[b]
        for l in range(depth):
            xb = _ffn_call(xb, ffn1_norm[l][None, :], ffn1_w_gate, ffn1_w_up, ffn1_w_down, l)
            wb = bf(w_in[l])
            win = jnp.concatenate(
                [wb[:, :off_g], wb[:, off_p:], jnp.pad(wb[:, off_g:off_p], ((0, 0), (0, LANES - 2 * N_HEADS)))], axis=1)
            bgate = jnp.pad(b_gates[l], (0, LANES - 2 * N_HEADS))[None, :]
            xb = _mixer_call(
                xb, mix_norm[l][None, :], win, bgate, conv_w[l], conv_b[l][None, :], mh_norm[l][None, :],
                bf(pool_w[l]), pool_scale[l][None, :], bf(w_out[l]))
            last = l == depth - 1
            xb = _ffn_call(xb, ffn2_norm[l][None, :], ffn2_w_gate, ffn2_w_up, ffn2_w_down, l,
                           final_g=final_norm[None, :] if last else None)
        outs.append(xb)
    return jnp.stack(outs, axis=0)
```

```python
import functools

import jax
import jax.numpy as jnp
from jax import lax
from jax.experimental import pallas as pl
from jax.experimental.pallas import tpu as pltpu

EPS = 1e-6
N_HEADS = 4
HEAD_DIM = 128
POOL_WINDOWS = (2, 4, 8, 16)
POOL_GROUP_DIM = 128
CONV_WIDTH = 4

LANES = 128
HALO = 16
V7X_VMEM_BYTES = 64 * 1024 * 1024

FFN_ROWS = 512
FFN_SUBTILES = 2
FFN_COLS = 256
WEIGHT_STAGE_SPLIT = 8
WEIGHT_STAGE_DEPTH = 4
MIX_STAGE_DEPTH = 4
MIX_ROWS = 512
MIX_CHUNK = 256
CONV_BLOCK = 32
POOL_BLOCK = 128

_F32 = jnp.float32
_BF16 = jnp.bfloat16


def _rmsnorm(x, g):
    y = x * lax.rsqrt(jnp.mean(x * x, axis=-1, keepdims=True) + EPS)
    return y * g


def _const_spec(shape):
    return pl.BlockSpec(shape, lambda i: (0,) * len(shape), pipeline_mode=pl.Buffered(1))


def _vmem_limit(nbytes):
    return int(min(nbytes * 1.25 + (8 << 20), V7X_VMEM_BYTES - (4 << 20)))


def _stage_weights(jobs, depth):
    copies = [pltpu.make_async_copy(src, stage, sem) for src, stage, sem, _ in jobs]
    for i in range(min(depth - 1, len(jobs))):
        copies[i].start()
    for i, (_, stage, _, land) in enumerate(jobs):
        if i + depth - 1 < len(jobs):
            copies[i + depth - 1].start()
        copies[i].wait()
        land(stage)


def _cast_into(dst):
    def land(stage):
        dst[...] = stage[...].astype(_BF16)
    return land


def _ffn_kernel(x_ref, g_ref, wg_hbm, wu_hbm, wd_hbm, *rest, cols, final, layer):
    if final:
        fg_ref, o_ref, h_scr, act_scr, wg_ref, wu_ref, wd_ref, stage_in, stage_out, sems = rest
    else:
        o_ref, h_scr, act_scr, wg_ref, wu_ref, wd_ref, stage_in, stage_out, sems = rest
    d, d_ff = wg_ref.shape

    @pl.when(pl.program_id(0) == 0)
    def _():
        jobs = []
        depth = stage_in.shape[0]
        for w_hbm, w_ref, stage, sem0 in ((wg_hbm, wg_ref, stage_in, 0), (wu_hbm, wu_ref, stage_in, 0),
                                          (wd_hbm, wd_ref, stage_out, depth)):
            chunk = stage.shape[1]
            for c in range(w_ref.shape[0] // chunk):
                rows = pl.ds(c * chunk, chunk)
                slot = len(jobs) % depth
                jobs.append((w_hbm.at[layer, rows], stage.at[slot], sems.at[sem0 + slot], _cast_into(w_ref.at[rows])))
        _stage_weights(jobs, depth)

    n_sub, rows = h_scr.shape[0], h_scr.shape[1]
    for t in range(n_sub):
        h_scr[t] = _rmsnorm(x_ref[t * rows:(t + 1) * rows, :], g_ref[...]).astype(_BF16)
    for t in range(n_sub):
        for j in range(d_ff // cols):
            sl = slice(j * cols, (j + 1) * cols)
            h = h_scr[t]
            gate = jnp.dot(h, wg_ref[:, sl], preferred_element_type=_F32)
            up = jnp.dot(h, wu_ref[:, sl], preferred_element_type=_F32)
            act_scr[t, :, sl] = (gate * jax.nn.sigmoid(gate) * up).astype(_BF16)
        down = jnp.dot(act_scr[t], wd_ref[...], preferred_element_type=_F32)
        y = x_ref[t * rows:(t + 1) * rows, :] + 0.5 * down
        if final:
            y = _rmsnorm(y, fg_ref[...])
        o_ref[t * rows:(t + 1) * rows, :] = y


def _ffn_call(x, g, wg, wu, wd, layer, final_g=None):
    s, d = x.shape
    d_ff = wg.shape[2]
    rows, cols, n_sub = FFN_ROWS, FFN_COLS, FFN_SUBTILES
    block = n_sub * rows
    in_chunk, out_chunk = d // WEIGHT_STAGE_SPLIT, d_ff // WEIGHT_STAGE_SPLIT
    assert s % block == 0 and d_ff % cols == 0
    assert d % in_chunk == 0 and d_ff % out_chunk == 0 and in_chunk % 16 == 0 and out_chunk % 16 == 0
    assert WEIGHT_STAGE_SPLIT % WEIGHT_STAGE_DEPTH == 0
    final = final_g is not None
    row_spec = pl.BlockSpec((block, d), lambda i: (i, 0))
    hbm_spec = pl.BlockSpec(memory_space=pl.ANY)
    in_specs = [row_spec, _const_spec((1, d)), hbm_spec, hbm_spec, hbm_spec]
    args = [x, g, wg, wu, wd]
    if final:
        in_specs.append(_const_spec((1, d)))
        args.append(final_g)
    vmem = (3 * d * d_ff * 2 + WEIGHT_STAGE_DEPTH * (in_chunk * d_ff + out_chunk * d) * 4 + 4 * block * d * 4 + block * d * 2
            + block * d_ff * 2 + 4 * rows * cols * 4)
    return pl.pallas_call(
        functools.partial(_ffn_kernel, cols=cols, final=final, layer=layer),
        grid=(s // block,),
        in_specs=in_specs,
        out_specs=row_spec,
        out_shape=jax.ShapeDtypeStruct((s, d), _F32),
        scratch_shapes=[
            pltpu.VMEM((n_sub, rows, d), _BF16), pltpu.VMEM((n_sub, rows, d_ff), _BF16),
            pltpu.VMEM((d, d_ff), _BF16), pltpu.VMEM((d, d_ff), _BF16), pltpu.VMEM((d_ff, d), _BF16),
            pltpu.VMEM((WEIGHT_STAGE_DEPTH, in_chunk, d_ff), _F32),
            pltpu.VMEM((WEIGHT_STAGE_DEPTH, out_chunk, d), _F32),
            pltpu.SemaphoreType.DMA((2 * WEIGHT_STAGE_DEPTH,))],
        compiler_params=pltpu.CompilerParams(dimension_semantics=("arbitrary",),
                                             vmem_limit_bytes=_vmem_limit(vmem)),
        name="swiglu_final" if final else "swiglu",
    )(*args)


def _gate_terms(gates, m_in, chunk):
    rows = gates.shape[0]
    g8 = gates.T[0:2 * N_HEADS, :]
    ig = g8[0:N_HEADS, :]
    t_in_chunk = lax.broadcasted_iota(jnp.int32, ig.shape, 1) % chunk
    b = jax.nn.log_sigmoid(g8[N_HEADS:2 * N_HEADS, :])
    shift = 1
    while shift < chunk:
        b = b + jnp.where(t_in_chunk >= shift, pltpu.roll(b, shift, axis=1), 0.0)
        shift *= 2
    drow = ig - b
    cm = drow
    shift = 1
    while shift < chunk:
        cm = jnp.maximum(cm, jnp.where(t_in_chunk >= shift, pltpu.roll(cm, shift, axis=1), -jnp.inf))
        shift *= 2
    neg_mx, inter_w, exp_neg_m, w_state, decays = [], [], [], [], []
    m_prev = m_in
    for c in range(rows // chunk):
        sl = slice(c * chunk, (c + 1) * chunk)
        g = b[:, (c + 1) * chunk - 1:(c + 1) * chunk]
        a = g + drow[:, sl]
        m_new = jnp.maximum(g + m_prev, jnp.max(a, axis=1, keepdims=True))
        mx = jnp.maximum(m_prev, cm[:, sl])
        neg_mx.append(-mx)
        inter_w.append(jnp.exp(m_prev - mx))
        exp_neg_m.append(jnp.exp(-(b[:, sl] + mx)))
        w_state.append(jnp.exp(a - m_new))
        decays.append(jnp.exp(g + m_prev - m_new))
        m_prev = m_new
    cat = lambda parts: jnp.concatenate(parts, axis=1)
    stacked = jnp.concatenate(
        [cat(neg_mx), cat(inter_w), cat(exp_neg_m), cat(w_state), jnp.zeros((LANES - 4 * N_HEADS, rows), _F32)],
        axis=0)
    return stacked.T, drow, decays, m_prev


def _mlstm_chunk(q, k, v, neg_mx, inter_w, exp_neg_m, w_state, d_row, decay, c_prev, n_prev):
    length = q.shape[0]
    qb, kb, vb = q.astype(_BF16), k.astype(_BF16), v.astype(_BF16)
    row_id = lax.broadcasted_iota(jnp.int32, (length, length), 0)
    col_id = lax.broadcasted_iota(jnp.int32, (length, length), 1)
    dw = jnp.exp(jnp.where(row_id >= col_id, neg_mx + d_row, -jnp.inf))
    scores = lax.dot_general(qb, kb, (((1,), (1,)), ((), ())), preferred_element_type=_F32) * dw
    num = (jnp.dot(scores.astype(_BF16), vb, preferred_element_type=_F32)
           + inter_w * jnp.dot(qb, c_prev.astype(_BF16), preferred_element_type=_F32))
    qn = (jnp.sum(scores, axis=1, keepdims=True)
          + inter_w * jnp.sum(q * n_prev, axis=1, keepdims=True))
    h = num / jnp.maximum(jnp.abs(qn), exp_neg_m)
    kw = k * w_state
    c_new = decay * c_prev + lax.dot_general(kw.astype(_BF16), vb, (((0,), (0,)), ((), ())),
                                             preferred_element_type=_F32)
    n_new = decay * n_prev + jnp.sum(kw, axis=0, keepdims=True)
    return h, c_new, n_new


def _regroup_w_in_into(dst):
    width = N_HEADS * HEAD_DIM
    off_gate = 4 * width
    off_pool = off_gate + 2 * N_HEADS
    pool_cols = len(POOL_WINDOWS) * POOL_GROUP_DIM

    def land(stage):
        dst[:, 0:off_gate] = stage[:, 0:off_gate].astype(_BF16)
        strip = stage[:, off_gate:off_gate + LANES]
        lane = lax.broadcasted_iota(jnp.int32, strip.shape, 1)
        dst[:, off_gate:off_gate + LANES] = jnp.where(lane < 2 * N_HEADS, strip, 0.0).astype(_BF16)
        tail = stage[:, off_gate:off_pool + pool_cols]
        dst[:, off_gate + LANES:off_gate + LANES + pool_cols] = tail[:, off_pool - off_gate:].astype(_BF16)
    return land


def _project(x, g_ref, win_ref, proj_ref):
    hb = _rmsnorm(x, g_ref[...]).astype(_BF16)
    proj_ref[HALO:, :] = jnp.dot(hb, win_ref[...], preferred_element_type=_F32)


def _mix_tile(proj_ref, x_ref, o_ref, row0, first_pos, *, bgate_ref, convw_ref, convb_ref, mhn_ref, poolw_ref,
              pscale_ref, wout_ref, qk_scr, mix_scr, c_scr, n_scr, m_scr, rows, chunk):
    width = N_HEADS * HEAD_DIM
    off_gate = 4 * width
    off_pool = off_gate + LANES

    for rb in range(rows // CONV_BLOCK):
        base = HALO + rb * CONV_BLOCK
        conv = convb_ref[...] + convw_ref[CONV_WIDTH - 1:CONV_WIDTH, :] * proj_ref[base:base + CONV_BLOCK, 0:2 * width]
        for j in range(CONV_WIDTH - 1):
            off = base - (CONV_WIDTH - 1) + j
            conv = conv + convw_ref[j:j + 1, :] * proj_ref[off:off + CONV_BLOCK, 0:2 * width]
        qk_scr[rb * CONV_BLOCK:(rb + 1) * CONV_BLOCK, :] = conv * jax.nn.sigmoid(conv)

    gates = proj_ref[HALO:, off_gate:off_gate + LANES] + bgate_ref[...]
    cols, drow, decays, m_out = _gate_terms(gates, m_scr[0:N_HEADS, 0:1], chunk)
    m_scr[0:N_HEADS, 0:1] = m_out

    for c in range(rows // chunk):
        rs = slice(c * chunk, (c + 1) * chunk)
        rh = slice(HALO + c * chunk, HALO + (c + 1) * chunk)
        for hd in range(N_HEADS):
            hs = slice(hd * HEAD_DIM, (hd + 1) * HEAD_DIM)
            q = qk_scr[rs, hs]
            k = qk_scr[rs, width + hd * HEAD_DIM: width + (hd + 1) * HEAD_DIM] * (HEAD_DIM ** -0.5)
            v = proj_ref[rh, 2 * width + hd * HEAD_DIM: 2 * width + (hd + 1) * HEAD_DIM]
            og = proj_ref[rh, 3 * width + hd * HEAD_DIM: 3 * width + (hd + 1) * HEAD_DIM]
            h, c_new, n_new = _mlstm_chunk(
                q, k, v,
                cols[rs, hd:hd + 1], cols[rs, N_HEADS + hd:N_HEADS + hd + 1],
                cols[rs, 2 * N_HEADS + hd:2 * N_HEADS + hd + 1], cols[rs, 3 * N_HEADS + hd:3 * N_HEADS + hd + 1],
                drow[hd:hd + 1, rs], decays[c][hd:hd + 1, :], c_scr[hd], n_scr[hd])
            c_scr[hd] = c_new
            n_scr[hd] = n_new
            h = h * lax.rsqrt(jnp.mean(h * h, axis=-1, keepdims=True) + EPS)
            mix_scr[rs, hs] = (jax.nn.sigmoid(og) * (h * mhn_ref[:, hs])).astype(_BF16)

    for rb in range(rows // POOL_BLOCK):
        r0 = rb * POOL_BLOCK
        pos = first_pos + r0 + lax.broadcasted_iota(jnp.int32, (POOL_BLOCK, 1), 0) + 1
        for gi, win in enumerate(POOL_WINDOWS):
            cs = slice(off_pool + gi * POOL_GROUP_DIM, off_pool + (gi + 1) * POOL_GROUP_DIM)
            total = proj_ref[r0:r0 + HALO + POOL_BLOCK, cs]
            span = 1
            while span < win:
                total = total + pltpu.roll(total, span, axis=0)
                span *= 2
            inv_count = 1.0 / jnp.minimum(pos, win).astype(_F32)
            pooled = total[HALO:, :] * inv_count - proj_ref[HALO + r0:HALO + r0 + POOL_BLOCK, cs]
            mixed = jnp.dot(pooled.astype(_BF16), poolw_ref[gi], preferred_element_type=_F32)
            mix_scr[r0:r0 + POOL_BLOCK, width + gi * POOL_GROUP_DIM: width + (gi + 1) * POOL_GROUP_DIM] = (
                mixed * pscale_ref[:, gi * POOL_GROUP_DIM:(gi + 1) * POOL_GROUP_DIM]).astype(_BF16)

    o_ref[row0:row0 + rows, :] = x_ref[row0:row0 + rows, :] + jnp.dot(
        mix_scr[...], wout_ref[...], preferred_element_type=_F32)


def _mixer_kernel(x_ref, xn_ref, g_ref, win_hbm, bgate_ref, convw_ref, convb_ref, mhn_ref, poolw_f32, pscale_ref,
                  wout_hbm, o_ref, proj_a, proj_b, qk_scr, mix_scr, c_scr, n_scr, m_scr, win_ref, wout_ref, poolw_ref,
                  stage_win, stage_wout, sems, *, chunk, layer):
    rows = xn_ref.shape[0]
    step = pl.program_id(0)
    tile = functools.partial(
        _mix_tile, bgate_ref=bgate_ref, convw_ref=convw_ref, convb_ref=convb_ref, mhn_ref=mhn_ref,
        poolw_ref=poolw_ref, pscale_ref=pscale_ref, wout_ref=wout_ref, qk_scr=qk_scr, mix_scr=mix_scr,
        c_scr=c_scr, n_scr=n_scr, m_scr=m_scr, rows=rows, chunk=chunk)

    @pl.when(step == 0)
    def _():
        proj_a[0:HALO, :] = jnp.zeros((HALO, proj_a.shape[1]), _F32)
        c_scr[...] = jnp.zeros(c_scr.shape, _F32)
        n_scr[...] = jnp.zeros(n_scr.shape, _F32)
        m_scr[...] = jnp.zeros(m_scr.shape, _F32)
        jobs = []
        depth = stage_win.shape[0]
        n_rows = stage_win.shape[1]
        for c in range(win_ref.shape[0] // n_rows):
            rs = pl.ds(c * n_rows, n_rows)
            slot = len(jobs) % depth
            jobs.append((win_hbm.at[layer, rs], stage_win.at[slot], sems.at[slot], _regroup_w_in_into(win_ref.at[rs])))
        for c in range(wout_ref.shape[0] // n_rows):
            rs = pl.ds(c * n_rows, n_rows)
            slot = len(jobs) % depth
            jobs.append((wout_hbm.at[layer, rs], stage_wout.at[slot], sems.at[depth + slot], _cast_into(wout_ref.at[rs])))
        _stage_weights(jobs, depth)
        poolw_ref[...] = poolw_f32[...].astype(_BF16)
        _project(x_ref[0:rows, :], g_ref, win_ref, proj_a)

    _project(x_ref[rows:2 * rows, :], g_ref, win_ref, proj_b)
    proj_b[0:HALO, :] = proj_a[rows:rows + HALO, :]
    tile(proj_a, x_ref, o_ref, 0, step * (2 * rows))
    _project(xn_ref[...], g_ref, win_ref, proj_a)
    proj_a[0:HALO, :] = proj_b[rows:rows + HALO, :]
    tile(proj_b, x_ref, o_ref, rows, step * (2 * rows) + rows)


def _mixer_call(x, g, w_in, bgate, convw, convb, mhn, poolw, pscale, w_out, layer):
    s, d = x.shape
    rows, chunk = MIX_ROWS, MIX_CHUNK
    width = N_HEADS * HEAD_DIM
    cols = w_in.shape[2] + LANES - 2 * N_HEADS
    d_mix, d_out = w_out.shape[1], w_out.shape[2]
    w_rows = d // WEIGHT_STAGE_SPLIT
    assert s % (2 * rows) == 0 and rows % chunk == 0 and chunk % LANES == 0
    assert rows % CONV_BLOCK == 0 and rows % POOL_BLOCK == 0
    assert d % w_rows == 0 and d_mix % w_rows == 0 and w_rows % 16 == 0
    assert (d // w_rows) % MIX_STAGE_DEPTH == 0
    n_steps = s // (2 * rows)
    pair_spec = pl.BlockSpec((2 * rows, d), lambda i: (i, 0))
    next_spec = pl.BlockSpec((rows, d), lambda i: (jnp.minimum(2 * i + 2, 2 * n_steps - 1), 0))
    hbm_spec = pl.BlockSpec(memory_space=pl.ANY)
    args = [g, w_in, bgate, convw, convb, mhn, poolw, pscale, w_out]
    in_specs = [pair_spec, next_spec] + [
        hbm_spec if a is w_in or a is w_out else _const_spec(a.shape) for a in args]
    vmem = ((d * cols + d_mix * d_out) * 2 + MIX_STAGE_DEPTH * w_rows * (cols + d_out) * 4 + 10 * rows * d * 4
            + 2 * (rows + HALO) * cols * 4 + rows * 2 * width * 6 + 4 * rows * cols * 4)
    return pl.pallas_call(
        functools.partial(_mixer_kernel, chunk=chunk, layer=layer),
        grid=(n_steps,),
        in_specs=in_specs,
        out_specs=pair_spec,
        out_shape=jax.ShapeDtypeStruct((s, d), _F32),
        scratch_shapes=[
            pltpu.VMEM((rows + HALO, cols), _F32),
            pltpu.VMEM((rows + HALO, cols), _F32),
            pltpu.VMEM((rows, 2 * width), _F32),
            pltpu.VMEM((rows, 2 * width), _BF16),
            pltpu.VMEM((N_HEADS, HEAD_DIM, HEAD_DIM), _F32),
            pltpu.VMEM((N_HEADS, 1, HEAD_DIM), _F32),
            pltpu.VMEM((8, LANES), _F32),
            pltpu.VMEM((d, cols), _BF16),
            pltpu.VMEM((d_mix, d_out), _BF16),
            pltpu.VMEM(poolw.shape, _BF16),
            pltpu.VMEM((MIX_STAGE_DEPTH, w_rows, w_in.shape[2]), _F32),
            pltpu.VMEM((MIX_STAGE_DEPTH, w_rows, d_out), _F32),
            pltpu.SemaphoreType.DMA((2 * MIX_STAGE_DEPTH,)),
        ],
        compiler_params=pltpu.CompilerParams(dimension_semantics=("arbitrary",),
                                             vmem_limit_bytes=_vmem_limit(vmem)),
        name="mixer",
    )(x, x, *args)


def kernel(x, ffn1_norm, ffn1_w_gate, ffn1_w_up, ffn1_w_down, mix_norm, w_in, b_gates, conv_w, conv_b,
           mh_norm, pool_w, pool_scale, w_out, ffn2_norm, ffn2_w_gate, ffn2_w_up, ffn2_w_down, final_norm):
    bsz, s, d = x.shape
    depth = ffn1_norm.shape[0]
    assert depth >= 1
    outs = []
    for b in range(bsz):
        xb = x[b]
        for l in range(depth):
            xb = _ffn_call(xb, ffn1_norm[l][None, :], ffn1_w_gate, ffn1_w_up, ffn1_w_down, l)
            bgate = jnp.pad(b_gates[l], (0, LANES - 2 * N_HEADS))[None, :]
            xb = _mixer_call(
                xb, mix_norm[l][None, :], w_in, bgate, conv_w[l], conv_b[l][None, :], mh_norm[l][None, :],
                pool_w[l], pool_scale[l][None, :], w_out, l)
            last = l == depth - 1
            xb = _ffn_call(xb, ffn2_norm[l][None, :], ffn2_w_gate, ffn2_w_up, ffn2_w_down, l,
                           final_g=final_norm[None, :] if last else None)
        outs.append(xb)
    return jnp.stack(outs, axis=0)
```
